```python
import math
import jax, jax.numpy as jnp
from jax import lax
import numpy as np

D_MODEL = 2048
BATCH = 4
SEQ = 4096
DEPTH = 2

N_MIXERS = 2
N_CONV_LAYERS = (DEPTH + 1) // 2
N_DN_LAYERS = DEPTH // 2

CONV_KERNEL = 31

DN_HEAD_K = 128
DN_HEAD_V = 128
DN_NUM_K_HEADS = D_MODEL // DN_HEAD_K
DN_NUM_V_HEADS = 2 * DN_NUM_K_HEADS
DN_KEY_DIM = DN_NUM_K_HEADS * DN_HEAD_K
DN_VALUE_DIM = DN_NUM_V_HEADS * DN_HEAD_V
DN_QKV_DIM = 2 * DN_KEY_DIM + DN_VALUE_DIM
DN_IN_DIM = DN_QKV_DIM + DN_VALUE_DIM + 4 * DN_NUM_V_HEADS
DN_SHORT_CONV = 5
CHUNK = 64

FFN_HIDDEN = ((8 * D_MODEL // 3 + 255) // 256) * 256

RMS_EPS = 1e-6
LN_EPS = 1e-5

kernel_name = "hybrid_conformer_gdn_encoder"


def rmsnorm(x, w, eps=RMS_EPS):
    xf = x.astype(jnp.float32)
    y = xf * lax.rsqrt(jnp.mean(xf * xf, axis=-1, keepdims=True) + eps)
    return (y * w.astype(jnp.float32)).astype(x.dtype)


def layernorm(x, g, b, eps=LN_EPS):
    xf = x.astype(jnp.float32)
    mu = jnp.mean(xf, axis=-1, keepdims=True)
    xc = xf - mu
    var = jnp.mean(xc * xc, axis=-1, keepdims=True)
    return (xc * lax.rsqrt(var + eps) * g.astype(jnp.float32) + b.astype(jnp.float32)).astype(x.dtype)


def l2norm(x, eps=1e-6):
    xf = x.astype(jnp.float32)
    return xf * lax.rsqrt(jnp.sum(xf * xf, axis=-1, keepdims=True) + eps)


def depthwise_conv_centred(x, w):
    width = w.shape[0]
    pad = (width - 1) // 2
    return lax.conv_general_dilated(
        x, w[:, None, :].astype(x.dtype), window_strides=(1,), padding=[(pad, width - 1 - pad)],
        dimension_numbers=("NWC", "WIO", "NWC"), feature_group_count=x.shape[-1])


def conformer_conv_module(h, w_pw1, b_pw1, w_dw, b_dw, ln_g, ln_b, w_pw2, b_pw2):
    u = h @ w_pw1 + b_pw1
    a, gate = jnp.split(u, 2, axis=-1)
    u = a * jax.nn.sigmoid(gate)
    u = depthwise_conv_centred(u, w_dw) + b_dw
    u = jax.nn.silu(layernorm(u, ln_g, ln_b))
    return u @ w_pw2 + b_pw2


def chunk_gated_delta_rule(q, k, v, g, beta):
    B, S, H, Dk = q.shape
    Dv = v.shape[-1]
    NC = S // CHUNK

    def to_chunks(t):
        return t.reshape(B, NC, CHUNK, H, t.shape[-1]).transpose(1, 0, 3, 2, 4)

    q = to_chunks(q.astype(jnp.float32)) * (Dk ** -0.5)
    k = to_chunks(k.astype(jnp.float32))
    v = to_chunks(v.astype(jnp.float32))
    beta = beta.astype(jnp.float32).reshape(B, NC, CHUNK, H).transpose(1, 0, 3, 2)
    g = jnp.cumsum(g.astype(jnp.float32).reshape(B, NC, CHUNK, H).transpose(1, 0, 3, 2), axis=-1)

    tril = jnp.tril(jnp.ones((CHUNK, CHUNK), dtype=bool))
    strict = jnp.tril(jnp.ones((CHUNK, CHUNK), dtype=bool), k=-1)
    diff = g[..., :, None] - g[..., None, :]
    decay = jnp.where(tril, jnp.exp(jnp.where(tril, diff, 0.0)), 0.0)

    kk = jnp.einsum("nbhcd,nbhed->nbhce", k, k)
    lower = jnp.where(strict, kk * decay * beta[..., :, None], 0.0)
    eye = jnp.eye(CHUNK, dtype=jnp.float32)
    tmat = lower + eye
    rhs = jnp.concatenate([v * beta[..., None], k * (beta * jnp.exp(g))[..., None]], axis=-1)
    sol = lax.linalg.triangular_solve(tmat, rhs, left_side=True, lower=True, unit_diagonal=True)
    u, w = sol[..., :Dv], sol[..., Dv:]

    a_qk = jnp.where(tril, jnp.einsum("nbhcd,nbhed->nbhce", q, k) * decay, 0.0)
    g_last = g[..., -1:]
    q_g = q * jnp.exp(g)[..., None]
    k_g = k * jnp.exp(g_last - g)[..., None]
    decay_last = jnp.exp(g_last)[..., None]

    def step(state, inp):
        q_c, k_c, u_c, w_c, a_c, d_c = inp
        v_new = u_c - jnp.einsum("bhcd,bhde->bhce", w_c, state)
        o_c = jnp.einsum("bhcd,bhde->bhce", q_c, state) + jnp.einsum("bhce,bhef->bhcf", a_c, v_new)
        state = state * d_c + jnp.einsum("bhcd,bhce->bhde", k_c, v_new)
        return state, o_c

    state0 = jnp.zeros((B, H, Dk, Dv), dtype=jnp.float32)
    _, o = lax.scan(step, state0, (q_g, k_g, u, w, a_qk, decay_last))
    return o.transpose(1, 0, 3, 2, 4).reshape(B, S, H, Dv)


def gated_deltanet_bidir(h, w_in, w_conv, a_log, dt_bias, norm_w, w_out):
    B, S, _ = h.shape
    proj = h @ w_in
    qkv = proj[..., :DN_QKV_DIM]
    z = proj[..., DN_QKV_DIM:DN_QKV_DIM + DN_VALUE_DIM]
    beta_raw = proj[..., DN_QKV_DIM + DN_VALUE_DIM:DN_QKV_DIM + DN_VALUE_DIM + 2 * DN_NUM_V_HEADS]
    a_raw = proj[..., DN_QKV_DIM + DN_VALUE_DIM + 2 * DN_NUM_V_HEADS:]

    qkv = jax.nn.silu(depthwise_conv_centred(qkv, w_conv))
    q = qkv[..., :DN_KEY_DIM].reshape(B, S, DN_NUM_K_HEADS, DN_HEAD_K)
    k = qkv[..., DN_KEY_DIM:2 * DN_KEY_DIM].reshape(B, S, DN_NUM_K_HEADS, DN_HEAD_K)
    v = qkv[..., 2 * DN_KEY_DIM:].reshape(B, S, DN_NUM_V_HEADS, DN_HEAD_V)
    rep = DN_NUM_V_HEADS // DN_NUM_K_HEADS
    q = jnp.repeat(l2norm(q), rep, axis=2)
    k = jnp.repeat(l2norm(k), rep, axis=2)

    beta = jax.nn.sigmoid(beta_raw.astype(jnp.float32)).reshape(B, S, 2, DN_NUM_V_HEADS)
    g = -jnp.exp(a_log.astype(jnp.float32)) * jax.nn.softplus(
        a_raw.astype(jnp.float32).reshape(B, S, 2, DN_NUM_V_HEADS) + dt_bias.astype(jnp.float32))

    o_fwd = chunk_gated_delta_rule(q, k, v, g[:, :, 0], beta[:, :, 0])
    flip = lambda t: jnp.flip(t, axis=1)
    o_bwd = flip(chunk_gated_delta_rule(flip(q), flip(k), flip(v), flip(g[:, :, 1]), flip(beta[:, :, 1])))
    o = o_fwd + o_bwd

    zf = z.astype(jnp.float32).reshape(B, S, DN_NUM_V_HEADS, DN_HEAD_V)
    o = o * lax.rsqrt(jnp.mean(o * o, axis=-1, keepdims=True) + RMS_EPS) * norm_w.astype(jnp.float32) * jax.nn.silu(zf)
    return o.reshape(B, S, DN_VALUE_DIM).astype(h.dtype) @ w_out


def swiglu_ffn(h, w_gate_up, w_down):
    gu = h @ w_gate_up
    gate, up = gu[..., :FFN_HIDDEN], gu[..., FFN_HIDDEN:]
    return (jax.nn.silu(gate) * up) @ w_down


def setup_inputs(seed: int = 0) -> dict:
    key = jax.random.key(seed)
    ks = jax.random.split(key, 24)
    f32 = jnp.float32
    D = D_MODEL

    def normal(k, shape, scale):
        return jax.random.normal(k, shape, dtype=f32) * scale

    x = jax.random.normal(ks[0], (BATCH, SEQ, D), dtype=f32)
    mix_norm = 1.0 + normal(ks[1], (DEPTH, D), 0.02)
    ffn_norm = 1.0 + normal(ks[2], (DEPTH, D), 0.02)
    final_norm = 1.0 + normal(ks[3], (D,), 0.02)

    cv_w_pw1 = normal(ks[4], (N_CONV_LAYERS, D, 2 * D), D ** -0.5)
    cv_b_pw1 = normal(ks[5], (N_CONV_LAYERS, 2 * D), 0.01)
    cv_w_dw = normal(ks[6], (N_CONV_LAYERS, CONV_KERNEL, D), CONV_KERNEL ** -0.5)
    cv_b_dw = normal(ks[7], (N_CONV_LAYERS, D), 0.01)
    cv_ln_g = 1.0 + normal(ks[8], (N_CONV_LAYERS, D), 0.02)
    cv_ln_b = normal(ks[9], (N_CONV_LAYERS, D), 0.01)
    cv_w_pw2 = normal(ks[10], (N_CONV_LAYERS, D, D), D ** -0.5)
    cv_b_pw2 = normal(ks[11], (N_CONV_LAYERS, D), 0.01)

    dn_w_in = normal(ks[12], (N_DN_LAYERS, D, DN_IN_DIM), D ** -0.5)
    dn_w_conv = normal(ks[13], (N_DN_LAYERS, DN_SHORT_CONV, DN_QKV_DIM), DN_SHORT_CONV ** -0.5)
    dn_a_log = jnp.log(jax.random.uniform(ks[14], (N_DN_LAYERS, 2, DN_NUM_V_HEADS), dtype=f32, minval=1.0, maxval=16.0))
    dt = jnp.exp(jax.random.uniform(ks[15], (N_DN_LAYERS, 2, DN_NUM_V_HEADS), dtype=f32,
                                    minval=math.log(1e-3), maxval=math.log(1e-1)))
    dn_dt_bias = dt + jnp.log(-jnp.expm1(-dt))
    dn_norm_w = 1.0 + normal(ks[16], (N_DN_LAYERS, DN_HEAD_V), 0.02)
    dn_w_out = normal(ks[17], (N_DN_LAYERS, DN_VALUE_DIM, D), DN_VALUE_DIM ** -0.5)

    ffn_w_gate_up = normal(ks[18], (DEPTH, D, 2 * FFN_HIDDEN), D ** -0.5)
    ffn_w_down = normal(ks[19], (DEPTH, FFN_HIDDEN, D), FFN_HIDDEN ** -0.5)

    return {
        "x": x, "mix_norm": mix_norm, "ffn_norm": ffn_norm, "final_norm": final_norm,
        "cv_w_pw1": cv_w_pw1, "cv_b_pw1": cv_b_pw1, "cv_w_dw": cv_w_dw, "cv_b_dw": cv_b_dw,
        "cv_ln_g": cv_ln_g, "cv_ln_b": cv_ln_b, "cv_w_pw2": cv_w_pw2, "cv_b_pw2": cv_b_pw2,
        "dn_w_in": dn_w_in, "dn_w_conv": dn_w_conv, "dn_a_log": dn_a_log, "dn_dt_bias": dn_dt_bias,
        "dn_norm_w": dn_norm_w, "dn_w_out": dn_w_out,
        "ffn_w_gate_up": ffn_w_gate_up, "ffn_w_down": ffn_w_down,
    }


def reference(x, mix_norm, ffn_norm, final_norm,
              cv_w_pw1, cv_b_pw1, cv_w_dw, cv_b_dw, cv_ln_g, cv_ln_b, cv_w_pw2, cv_b_pw2,
              dn_w_in, dn_w_conv, dn_a_log, dn_dt_bias, dn_norm_w, dn_w_out,
              ffn_w_gate_up, ffn_w_down):
    h = x
    for i in range(DEPTH):
        j = i // N_MIXERS
        hn = rmsnorm(h, mix_norm[i])
        if i % N_MIXERS == 0:
            mix = conformer_conv_module(hn, cv_w_pw1[j], cv_b_pw1[j], cv_w_dw[j], cv_b_dw[j],
                                        cv_ln_g[j], cv_ln_b[j], cv_w_pw2[j], cv_b_pw2[j])
        else:
            mix = gated_deltanet_bidir(hn, dn_w_in[j], dn_w_conv[j], dn_a_log[j], dn_dt_bias[j],
                                       dn_norm_w[j], dn_w_out[j])
        h = h + mix.astype(h.dtype)
        h = h + swiglu_ffn(rmsnorm(h, ffn_norm[i]), ffn_w_gate_up[i], ffn_w_down[i]).astype(h.dtype)
    return rmsnorm(h, final_norm)
```

```python
import functools

import jax
import jax.numpy as jnp
from jax import lax
from jax.experimental import pallas as pl
from jax.experimental.pallas import tpu as pltpu

F32 = jnp.float32
BF16 = jnp.bfloat16

RMS_EPS = 1e-6
LN_EPS = 1e-5
L2_EPS = 1e-6

CONV_KERNEL = 31
CONV_HALO = 16
SHORT_CONV = 5
SHORT_HALO = 8

HEAD_DIM = 128
NUM_K_HEADS = 16
NUM_V_HEADS = 32
V_PER_K = NUM_V_HEADS // NUM_K_HEADS
CHUNK = 64

VMEM_LIMIT = 56 * 1024 * 1024


def _cparams(sem):
    return pltpu.CompilerParams(dimension_semantics=sem, vmem_limit_bytes=VMEM_LIMIT)


def _mm(a, b):
    return jnp.dot(a.astype(BF16), b.astype(BF16), preferred_element_type=F32)


def _mm_nt(a, b):
    return lax.dot_general(a.astype(BF16), b.astype(BF16), (((1,), (1,)), ((), ())),
                           preferred_element_type=F32)


def _mm_tn(a, b):
    return lax.dot_general(a.astype(BF16), b.astype(BF16), (((0,), (0,)), ((), ())),
                           preferred_element_type=F32)


def _rms(x, w):
    ms = jnp.mean(x * x, axis=-1, keepdims=True)
    return x * lax.rsqrt(ms + RMS_EPS) * w


def _sigmoid(x):
    return 1.0 / (1.0 + jnp.exp(-x))


def _norm_mm_kernel(h_ref, nw_ref, w_ref, b_ref, o_ref, hn_ref):
    @pl.when(pl.program_id(1) == 0)
    def _():
        hn_ref[...] = _rms(h_ref[...], nw_ref[...]).astype(BF16)

    o_ref[...] = (jnp.dot(hn_ref[...], w_ref[...], preferred_element_type=F32)
                  + b_ref[...]).astype(o_ref.dtype)


def _norm_mm(h, nw, w, b, *, tm, tn, out_dtype=F32):
    m, d = h.shape
    n = w.shape[1]
    return pl.pallas_call(
        _norm_mm_kernel,
        out_shape=jax.ShapeDtypeStruct((m, n), out_dtype),
        grid=(m // tm, n // tn),
        in_specs=[
            pl.BlockSpec((tm, d), lambda i, j: (i, 0)),
            pl.BlockSpec((1, d), lambda i, j: (0, 0)),
            pl.BlockSpec((d, tn), lambda i, j: (0, j)),
            pl.BlockSpec((1, tn), lambda i, j: (0, j)),
        ],
        out_specs=pl.BlockSpec((tm, tn), lambda i, j: (i, j)),
        scratch_shapes=[pltpu.VMEM((tm, d), BF16)],
        compiler_params=_cparams(("parallel", "arbitrary")),
        name="norm_mm",
    )(h, nw, w, b)


def _norm_glu_kernel(h_ref, nw_ref, wa_ref, wg_ref, ba_ref, bg_ref, o_ref, hn_ref):
    @pl.when(pl.program_id(1) == 0)
    def _():
        hn_ref[...] = _rms(h_ref[...], nw_ref[...]).astype(BF16)

    hn = hn_ref[...]
    a = jnp.dot(hn, wa_ref[...], preferred_element_type=F32) + ba_ref[...]
    g = jnp.dot(hn, wg_ref[...], preferred_element_type=F32) + bg_ref[...]
    o_ref[...] = a * _sigmoid(g)


def _norm_glu(h, nw, w, b, *, tm, tn):
    m, d = h.shape
    n = w.shape[1] // 2
    nb = n // tn
    return pl.pallas_call(
        _norm_glu_kernel,
        out_shape=jax.ShapeDtypeStruct((m, n), F32),
        grid=(m // tm, nb),
        in_specs=[
            pl.BlockSpec((tm, d), lambda i, j: (i, 0)),
            pl.BlockSpec((1, d), lambda i, j: (0, 0)),
            pl.BlockSpec((d, tn), lambda i, j: (0, j)),
            pl.BlockSpec((d, tn), lambda i, j: (0, j + nb)),
            pl.BlockSpec((1, tn), lambda i, j: (0, j)),
            pl.BlockSpec((1, tn), lambda i, j: (0, j + nb)),
        ],
        out_specs=pl.BlockSpec((tm, tn), lambda i, j: (i, j)),
        scratch_shapes=[pltpu.VMEM((tm, d), BF16)],
        compiler_params=_cparams(("parallel", "arbitrary")),
        name="norm_glu",
    )(h, nw, w, w, b, b)


_CONV_ROWS = 64
_CONV_LANES = 256


def _conv_module_kernel(uc_ref, up_ref, un_ref, h_ref, wdw_ref, bdw_ref, g_ref, b_ref, w2_ref, b2_ref,
                        o_ref, ext_ref, cv_ref):
    i = pl.program_id(1)
    tm = uc_ref.shape[1]
    d = uc_ref.shape[2]
    pad = (CONV_KERNEL - 1) // 2
    ext_ref[0:CONV_HALO, :] = jnp.where(i > 0, up_ref[0], 0.0)
    ext_ref[CONV_HALO:CONV_HALO + tm, :] = uc_ref[0]
    ext_ref[CONV_HALO + tm:, :] = jnp.where(i < pl.num_programs(1) - 1, un_ref[0], 0.0)

    base = CONV_HALO - pad
    def lane_block(cb, carry):
        cols = pl.ds(pl.multiple_of(cb * _CONV_LANES, _CONV_LANES), _CONV_LANES)
        for r0 in range(0, tm, _CONV_ROWS):
            acc = jnp.zeros((_CONV_ROWS, _CONV_LANES), F32)
            for j in range(CONV_KERNEL):
                acc = acc + wdw_ref[j:j + 1, cols] * ext_ref[r0 + base + j:r0 + base + j + _CONV_ROWS, cols]
            cv_ref[r0:r0 + _CONV_ROWS, cols] = acc + bdw_ref[:, cols]
        return carry

    lax.fori_loop(0, d // _CONV_LANES, lane_block, 0)

    x = cv_ref[...]
    mu = jnp.mean(x, axis=-1, keepdims=True)
    xc = x - mu
    var = jnp.mean(xc * xc, axis=-1, keepdims=True)
    y = xc * lax.rsqrt(var + LN_EPS) * g_ref[...] + b_ref[...]
    y = y * _sigmoid(y)
    o_ref[0] = h_ref[0] + jnp.dot(y.astype(BF16), w2_ref[...], preferred_element_type=F32) + b2_ref[...]


def _conv_module(u, h, wdw, bdw, ln_g, ln_b, w2, b2, *, tm):
    bsz, s, d = u.shape
    nh = tm // CONV_HALO
    last = s // CONV_HALO - 1
    tile = pl.BlockSpec((1, tm, d), lambda b, i: (b, i, 0))
    vec = pl.BlockSpec((1, d), lambda b, i: (0, 0))
    return pl.pallas_call(
        _conv_module_kernel,
        out_shape=jax.ShapeDtypeStruct((bsz, s, d), F32),
        grid=(bsz, s // tm),
        in_specs=[
            tile,
            pl.BlockSpec((1, CONV_HALO, d), lambda b, i: (b, jnp.maximum(i * nh - 1, 0), 0)),
            pl.BlockSpec((1, CONV_HALO, d), lambda b, i: (b, jnp.minimum((i + 1) * nh, last), 0)),
            tile,
            pl.BlockSpec((CONV_KERNEL, d), lambda b, i: (0, 0)),
            vec, vec, vec,
            pl.BlockSpec((d, d), lambda b, i: (0, 0), pipeline_mode=pl.Buffered(1)),
            vec,
        ],
        out_specs=tile,
        scratch_shapes=[pltpu.VMEM((tm + 2 * CONV_HALO, d), F32), pltpu.VMEM((tm, d), F32)],
        compiler_params=_cparams(("parallel", "parallel")),
        name="conv_module",
    )(u, u, u, h, wdw, bdw, ln_g, ln_b, w2, b2)


def _ffn_kernel(h_ref, nw_ref, wg_ref, wu_ref, wd_ref, fw_ref, o_ref, hn_ref, acc_ref, *, final_norm):
    j = pl.program_id(1)

    @pl.when(j == 0)
    def _():
        hn_ref[...] = _rms(h_ref[...], nw_ref[...]).astype(BF16)
        acc_ref[...] = jnp.zeros_like(acc_ref)

    hn = hn_ref[...]
    g = jnp.dot(hn, wg_ref[...], preferred_element_type=F32)
    u = jnp.dot(hn, wu_ref[...], preferred_element_type=F32)
    a = (g * _sigmoid(g) * u).astype(BF16)
    acc_ref[...] += jnp.dot(a, wd_ref[...], preferred_element_type=F32)

    @pl.when(j == pl.num_programs(1) - 1)
    def _():
        y = h_ref[...] + acc_ref[...]
        if final_norm:
            y = _rms(y, fw_ref[...])
        o_ref[...] = y


def _ffn(h, nw, w_gate_up, w_down, fw, *, tm, th, final_norm):
    m, d = h.shape
    hid = w_down.shape[0]
    nb = hid // th
    vec = pl.BlockSpec((1, d), lambda i, j: (0, 0))
    return pl.pallas_call(
        functools.partial(_ffn_kernel, final_norm=final_norm),
        out_shape=jax.ShapeDtypeStruct((m, d), F32),
        grid=(m // tm, nb),
        in_specs=[
            pl.BlockSpec((tm, d), lambda i, j: (i, 0)),
            vec,
            pl.BlockSpec((d, th), lambda i, j: (0, j)),
            pl.BlockSpec((d, th), lambda i, j: (0, j + nb)),
            pl.BlockSpec((th, d), lambda i, j: (j, 0)),
            vec,
        ],
        out_specs=pl.BlockSpec((tm, d), lambda i, j: (i, 0)),
        scratch_shapes=[pltpu.VMEM((tm, d), BF16), pltpu.VMEM((tm, d), F32)],
        compiler_params=_cparams(("parallel", "arbitrary")),
        name="ffn",
    )(h, nw, w_gate_up, w_gate_up, w_down, fw)


def _dn_prep_kernel(xc_ref, xp_ref, xn_ref, wc_ref, ba_ref, alog_ref, dtb_ref,
                    q_ref, k_ref, v_ref, gate_ref, ext_ref):
    i = pl.program_id(1)
    tm = xc_ref.shape[1]
    key_dim = q_ref.shape[2]
    pad = (SHORT_CONV - 1) // 2
    ext_ref[0:SHORT_HALO, :] = jnp.where(i > 0, xp_ref[0], 0.0)
    ext_ref[SHORT_HALO:SHORT_HALO + tm, :] = xc_ref[0]
    ext_ref[SHORT_HALO + tm:, :] = jnp.where(i < pl.num_programs(1) - 1, xn_ref[0], 0.0)

    base = SHORT_HALO - pad
    n_heads = xc_ref.shape[2] // HEAD_DIM
    for hd in range(n_heads):
        cols = slice(hd * HEAD_DIM, (hd + 1) * HEAD_DIM)
        acc = jnp.zeros((tm, HEAD_DIM), F32)
        for j in range(SHORT_CONV):
            acc = acc + wc_ref[j:j + 1, cols] * ext_ref[base + j:base + j + tm, cols]
        y = acc * _sigmoid(acc)
        c0 = hd * HEAD_DIM
        if c0 < 2 * key_dim:
            y = y * lax.rsqrt(jnp.sum(y * y, axis=-1, keepdims=True) + L2_EPS)
            if c0 < key_dim:
                q_ref[0, :, cols] = (y * (HEAD_DIM ** -0.5)).astype(BF16)
            else:
                k_ref[0, :, c0 - key_dim:c0 - key_dim + HEAD_DIM] = y.astype(BF16)
        else:
            v_ref[0, :, c0 - 2 * key_dim:c0 - 2 * key_dim + HEAD_DIM] = y.astype(BF16)

    raw = ba_ref[0]
    lane = lax.broadcasted_iota(jnp.int32, raw.shape, 1)
    beta = _sigmoid(raw)
    z = raw + dtb_ref[...]
    softplus = jnp.maximum(z, 0.0) + jnp.log(1.0 + jnp.exp(-jnp.abs(z)))
    g = -jnp.exp(alog_ref[...]) * softplus
    r = lax.broadcasted_iota(jnp.int32, (tm, tm), 0)
    c = lax.broadcasted_iota(jnp.int32, (tm, tm), 1)
    same = (r // CHUNK) == (c // CHUNK)
    lower = jnp.where(same & (c <= r), 1.0, 0.0).astype(F32)
    upper = jnp.where(same & (c >= r), 1.0, 0.0).astype(F32)
    gf = jnp.dot(lower, g, precision=lax.Precision.HIGHEST, preferred_element_type=F32)
    gb = jnp.dot(upper, g, precision=lax.Precision.HIGHEST, preferred_element_type=F32)
    quarter = HEAD_DIM // 4
    gate_ref[0] = jnp.where(lane < 2 * quarter, beta, jnp.where(lane < 3 * quarter, gf, gb))


def _dn_prep(proj, ba, w_conv, alog_pad, dtb_pad, *, tm, key_dim, value_dim):
    bsz, s, _ = proj.shape
    qkv_dim = 2 * key_dim + value_dim
    nh = tm // SHORT_HALO
    last = s // SHORT_HALO - 1
    return pl.pallas_call(
        _dn_prep_kernel,
        out_shape=[
            jax.ShapeDtypeStruct((bsz, s, key_dim), BF16),
            jax.ShapeDtypeStruct((bsz, s, key_dim), BF16),
            jax.ShapeDtypeStruct((bsz, s, value_dim), BF16),
            jax.ShapeDtypeStruct((bsz, s, HEAD_DIM), F32),
        ],
        grid=(bsz, s // tm),
        in_specs=[
            pl.BlockSpec((1, tm, qkv_dim), lambda b, i: (b, i, 0)),
            pl.BlockSpec((1, SHORT_HALO, qkv_dim), lambda b, i: (b, jnp.maximum(i * nh - 1, 0), 0)),
            pl.BlockSpec((1, SHORT_HALO, qkv_dim), lambda b, i: (b, jnp.minimum((i + 1) * nh, last), 0)),
            pl.BlockSpec((SHORT_CONV, qkv_dim), lambda b, i: (0, 0)),
            pl.BlockSpec((1, tm, HEAD_DIM), lambda b, i: (b, i, 0)),
            pl.BlockSpec((1, HEAD_DIM), lambda b, i: (0, 0)),
            pl.BlockSpec((1, HEAD_DIM), lambda b, i: (0, 0)),
        ],
        out_specs=[
            pl.BlockSpec((1, tm, key_dim), lambda b, i: (b, i, 0)),
            pl.BlockSpec((1, tm, key_dim), lambda b, i: (b, i, 0)),
            pl.BlockSpec((1, tm, value_dim), lambda b, i: (b, i, 0)),
            pl.BlockSpec((1, tm, HEAD_DIM), lambda b, i: (b, i, 0)),
        ],
        scratch_shapes=[pltpu.VMEM((tm + 2 * SHORT_HALO, qkv_dim), F32)],
        compiler_params=_cparams(("parallel", "parallel")),
        name="dn_prep",
    )(proj, proj, proj, w_conv, ba, alog_pad, dtb_pad)


def _unit_tri_inverse(low):
    n = low.shape[0]
    r = lax.broadcasted_iota(jnp.int32, (n, n), 0)
    c = lax.broadcasted_iota(jnp.int32, (n, n), 1)
    p = -low
    t = jnp.where(r == c, 1.0, 0.0) + p
    steps = n.bit_length() - 2
    for _ in range(steps):
        p = _mm(p, p)
        t = t + _mm(t, p)
    return t


def _delta_chain(q, k, kk, qk, v, gates, gates_t_ref, lane_beta, lane_g, state_ref, backward):
    cs = q.shape[0]
    lane = lax.broadcasted_iota(jnp.int32, gates.shape, 1)
    beta = jnp.sum(jnp.where(lane == lane_beta, gates, 0.0), axis=1, keepdims=True)
    gc = jnp.sum(jnp.where(lane == lane_g, gates, 0.0), axis=1, keepdims=True)
    gr = gates_t_ref[pl.ds(lane_g, 1), :]
    r = lax.broadcasted_iota(jnp.int32, (cs, cs), 0)
    c = lax.broadcasted_iota(jnp.int32, (cs, cs), 1)
    incl = (r <= c) if backward else (r >= c)
    strict = (r < c) if backward else (r > c)
    decay = jnp.where(incl, jnp.exp(jnp.where(incl, gc - gr, 0.0)), 0.0)
    t = _unit_tri_inverse(jnp.where(strict, kk * decay * beta, 0.0))
    eg = jnp.exp(gc)
    rhs = jnp.concatenate([v.astype(F32) * beta, k.astype(F32) * (beta * eg)], axis=-1)
    sol = _mm(t, rhs)
    dv = v.shape[1]
    u, w = sol[:, :dv], sol[:, dv:]
    a = jnp.where(incl, qk * decay, 0.0)
    g_last = gr[:, 0:1] if backward else gr[:, cs - 1:cs]
    q_g = q.astype(F32) * eg
    k_g = k.astype(F32) * jnp.exp(g_last - gc)
    state = state_ref[...]
    ws = _mm(jnp.concatenate([w, q_g], axis=0), state)
    v_new = u - ws[:cs]
    o = ws[cs:] + _mm(a, v_new)
    state_ref[...] = state * jnp.exp(g_last) + _mm_tn(k_g, v_new)
    return o


def _delta_kernel(qf_ref, kf_ref, vf_ref, gf_ref, qb_ref, kb_ref, vb_ref, gb_ref,
                  of_ref, ob_ref, state_ref, gt_ref):
    kh = pl.program_id(1)

    @pl.when(pl.program_id(2) == 0)
    def _():
        state_ref[...] = jnp.zeros_like(state_ref)

    quarter = HEAD_DIM // 4
    for d, (q_ref, k_ref, v_ref, g_ref, o_ref) in enumerate(
            ((qf_ref, kf_ref, vf_ref, gf_ref, of_ref), (qb_ref, kb_ref, vb_ref, gb_ref, ob_ref))):
        q = q_ref[0]
        k = k_ref[0]
        gates = g_ref[0]
        gt_ref[d] = gates.T
        kk = _mm_nt(k, k)
        qk = _mm_nt(q, k)
        for i in range(V_PER_K):
            head = kh * V_PER_K + i
            o = _delta_chain(q, k, kk, qk, v_ref[0, :, i * HEAD_DIM:(i + 1) * HEAD_DIM], gates, gt_ref.at[d],
                             d * quarter + head, (2 + d) * quarter + head,
                             state_ref.at[d * V_PER_K + i], backward=(d == 1))
            o_ref[0, :, i * HEAD_DIM:(i + 1) * HEAD_DIM] = o


def _delta_rule(q, k, v, gates):
    bsz, s, key_dim = q.shape
    value_dim = v.shape[2]
    nc = s // CHUNK
    vw = V_PER_K * HEAD_DIM
    fwd = lambda b, h, c: (b, c, h)
    bwd = lambda b, h, c: (b, nc - 1 - c, h)
    fwd0 = lambda b, h, c: (b, c, 0)
    bwd0 = lambda b, h, c: (b, nc - 1 - c, 0)
    qk_spec = lambda m: pl.BlockSpec((1, CHUNK, HEAD_DIM), m)
    v_spec = lambda m: pl.BlockSpec((1, CHUNK, vw), m)
    return pl.pallas_call(
        _delta_kernel,
        out_shape=[jax.ShapeDtypeStruct((bsz, s, value_dim), F32)] * 2,
        grid=(bsz, key_dim // HEAD_DIM, nc),
        in_specs=[qk_spec(fwd), qk_spec(fwd), v_spec(fwd), qk_spec(fwd0),
                  qk_spec(bwd), qk_spec(bwd), v_spec(bwd), qk_spec(bwd0)],
        out_specs=[v_spec(fwd), v_spec(bwd)],
        scratch_shapes=[pltpu.VMEM((2 * V_PER_K, HEAD_DIM, HEAD_DIM), F32),
                        pltpu.VMEM((2, HEAD_DIM, CHUNK), F32)],
        compiler_params=_cparams(("parallel", "parallel", "arbitrary")),
        name="delta_rule",
    )(q, k, v, gates, q, k, v, gates)


def _dn_out_kernel(of_ref, ob_ref, z_ref, nw_ref, w_ref, h_ref, o_ref, y_ref):
    for hd in range(of_ref.shape[1] // HEAD_DIM):
        cols = slice(hd * HEAD_DIM, (hd + 1) * HEAD_DIM)
        o = of_ref[:, cols] + ob_ref[:, cols]
        z = z_ref[:, cols]
        y = _rms(o, nw_ref[...]) * (z * _sigmoid(z))
        y_ref[:, cols] = y.astype(BF16)
    o_ref[...] = h_ref[...] + jnp.dot(y_ref[...], w_ref[...], preferred_element_type=F32)


def _dn_out(o_f, o_b, proj, z_block, norm_w, w_out, h, *, tm):
    m, vd = o_f.shape
    d = h.shape[1]
    return pl.pallas_call(
        _dn_out_kernel,
        out_shape=jax.ShapeDtypeStruct((m, d), F32),
        grid=(m // tm,),
        in_specs=[
            pl.BlockSpec((tm, vd), lambda i: (i, 0)),
            pl.BlockSpec((tm, vd), lambda i: (i, 0)),
            pl.BlockSpec((tm, vd), lambda i: (i, z_block)),
            pl.BlockSpec((1, HEAD_DIM), lambda i: (0, 0)),
            pl.BlockSpec((vd, d), lambda i: (0, 0), pipeline_mode=pl.Buffered(1)),
            pl.BlockSpec((tm, d), lambda i: (i, 0)),
        ],
        out_specs=pl.BlockSpec((tm, d), lambda i: (i, 0)),
        scratch_shapes=[pltpu.VMEM((tm, vd), BF16)],
        compiler_params=_cparams(("parallel",)),
        name="dn_out",
    )(o_f, o_b, proj, norm_w, w_out, h)


def _row(v):
    return v.reshape(1, -1).astype(F32)


def _conformer_layer(h, nw, w_pw1, b_pw1, w_dw, b_dw, ln_g, ln_b, w_pw2, b_pw2):
    bsz, s, d = h.shape
    u = _norm_glu(h.reshape(bsz * s, d), _row(nw), w_pw1.astype(BF16), _row(b_pw1), tm=512, tn=512)
    return _conv_module(u.reshape(bsz, s, d), h, w_dw, _row(b_dw), _row(ln_g), _row(ln_b),
                        w_pw2.astype(BF16), _row(b_pw2), tm=256)


def _deltanet_layer(h, nw, w_in, w_conv, a_log, dt_bias, norm_w, w_out):
    bsz, s, d = h.shape
    key_dim = NUM_K_HEADS * HEAD_DIM
    value_dim = NUM_V_HEADS * HEAD_DIM
    main = 2 * key_dim + value_dim + value_dim
    h2 = h.reshape(bsz * s, d)
    w_in = w_in.astype(BF16)
    proj = _norm_mm(h2, _row(nw), w_in[:, :main], jnp.zeros((1, main), F32), tm=512, tn=1024)
    ba = _norm_mm(h2, _row(nw), w_in[:, main:], jnp.zeros((1, w_in.shape[1] - main), F32), tm=512,
                  tn=w_in.shape[1] - main)
    zeros = jnp.zeros((2 * NUM_V_HEADS,), F32)
    alog_pad = jnp.concatenate([zeros, a_log.reshape(-1).astype(F32)]).reshape(1, -1)
    dtb_pad = jnp.concatenate([zeros, dt_bias.reshape(-1).astype(F32)]).reshape(1, -1)
    q, k, v, gates = _dn_prep(proj.reshape(bsz, s, main), ba.reshape(bsz, s, -1), w_conv, alog_pad, dtb_pad,
                              tm=256, key_dim=key_dim, value_dim=value_dim)
    o_f, o_b = _delta_rule(q, k, v, gates)
    out = _dn_out(o_f.reshape(bsz * s, value_dim), o_b.reshape(bsz * s, value_dim), proj,
                  (2 * key_dim + value_dim) // value_dim, _row(norm_w), w_out.astype(BF16), h2, tm=256)
    return out.reshape(bsz, s, d)


def kernel(x, mix_norm, ffn_norm, final_norm, cv_w_pw1, cv_b_pw1, cv_w_dw, cv_b_dw, cv_ln_g, cv_ln_b,
           cv_w_pw2, cv_b_pw2, dn_w_in, dn_w_conv, dn_a_log, dn_dt_bias, dn_norm_w, dn_w_out,
           ffn_w_gate_up, ffn_w_down):
    bsz, s, d = x.shape
    depth = mix_norm.shape[0]
    h = x
    for i in range(depth):
        j = i // 2
        if i % 2 == 0:
            h = _conformer_layer(h, mix_norm[i], cv_w_pw1[j], cv_b_pw1[j], cv_w_dw[j], cv_b_dw[j],
                                 cv_ln_g[j], cv_ln_b[j], cv_w_pw2[j], cv_b_pw2[j])
        else:
            h = _deltanet_layer(h, mix_norm[i], dn_w_in[j], dn_w_conv[j], dn_a_log[j], dn_dt_bias[j],
                                dn_norm_w[j], dn_w_out[j])
        last = i == depth - 1
        h = _ffn(h.reshape(bsz * s, d), _row(ffn_norm[i]), ffn_w_gate_up[i].astype(BF16),
                 ffn_w_down[i].astype(BF16), _row(final_norm), tm=512, th=512,
                 final_norm=last).reshape(bsz, s, d)
    return h
```

```python
import functools

import jax
import jax.numpy as jnp
from jax import lax
from jax.experimental import pallas as pl
from jax.experimental.pallas import tpu as pltpu

F32 = jnp.float32
BF16 = jnp.bfloat16

RMS_EPS = 1e-6
LN_EPS = 1e-5
L2_EPS = 1e-6

CONV_KERNEL = 31
CONV_HALO = 16
SHORT_CONV = 5
SHORT_HALO = 8

SUBLANES = 8
HEAD_DIM = 128
NUM_K_HEADS = 16
NUM_V_HEADS = 32
V_PER_K = NUM_V_HEADS // NUM_K_HEADS
CHUNK = 64

VMEM_LIMIT = 56 * 1024 * 1024


def _cparams(sem):
    return pltpu.CompilerParams(dimension_semantics=sem, vmem_limit_bytes=VMEM_LIMIT)


def _mm(a, b):
    return jnp.dot(a.astype(BF16), b.astype(BF16), preferred_element_type=F32)


def _mm_nt(a, b):
    return lax.dot_general(a.astype(BF16), b.astype(BF16), (((1,), (1,)), ((), ())),
                           preferred_element_type=F32)


def _rms(x, w):
    ms = jnp.mean(x * x, axis=-1, keepdims=True)
    return x * lax.rsqrt(ms + RMS_EPS) * w


def _sigmoid(x):
    return 1.0 / (1.0 + jnp.exp(-x))


def _norm_mm_kernel(h_ref, nw_ref, w_ref, b_ref, o_ref, hn_ref):
    @pl.when(pl.program_id(1) == 0)
    def _():
        hn_ref[...] = _rms(h_ref[...], nw_ref[...]).astype(BF16)

    o_ref[...] = (jnp.dot(hn_ref[...], w_ref[...], preferred_element_type=F32)
                  + b_ref[...]).astype(o_ref.dtype)


def _norm_mm(h, nw, w, b, *, tm, tn, out_dtype=F32):
    m, d = h.shape
    n = w.shape[1]
    return pl.pallas_call(
        _norm_mm_kernel,
        out_shape=jax.ShapeDtypeStruct((m, n), out_dtype),
        grid=(m // tm, n // tn),
        in_specs=[
            pl.BlockSpec((tm, d), lambda i, j: (i, 0)),
            pl.BlockSpec((1, d), lambda i, j: (0, 0)),
            pl.BlockSpec((d, tn), lambda i, j: (0, j)),
            pl.BlockSpec((1, tn), lambda i, j: (0, j)),
        ],
        out_specs=pl.BlockSpec((tm, tn), lambda i, j: (i, j)),
        scratch_shapes=[pltpu.VMEM((tm, d), BF16)],
        compiler_params=_cparams(("parallel", "arbitrary")),
        name="norm_mm",
    )(h, nw, w, b)


def _norm_glu_kernel(h_ref, nw_ref, wa_ref, wg_ref, ba_ref, bg_ref, o_ref, hn_ref):
    @pl.when(pl.program_id(1) == 0)
    def _():
        hn_ref[...] = _rms(h_ref[...], nw_ref[...]).astype(BF16)

    hn = hn_ref[...]
    a = jnp.dot(hn, wa_ref[...], preferred_element_type=F32) + ba_ref[...]
    g = jnp.dot(hn, wg_ref[...], preferred_element_type=F32) + bg_ref[...]
    o_ref[...] = a * _sigmoid(g)


def _norm_glu(h, nw, w, b, *, tm, tn):
    m, d = h.shape
    n = w.shape[1] // 2
    nb = n // tn
    return pl.pallas_call(
        _norm_glu_kernel,
        out_shape=jax.ShapeDtypeStruct((m, n), F32),
        grid=(m // tm, nb),
        in_specs=[
            pl.BlockSpec((tm, d), lambda i, j: (i, 0)),
            pl.BlockSpec((1, d), lambda i, j: (0, 0)),
            pl.BlockSpec((d, tn), lambda i, j: (0, j)),
            pl.BlockSpec((d, tn), lambda i, j: (0, j + nb)),
            pl.BlockSpec((1, tn), lambda i, j: (0, j)),
            pl.BlockSpec((1, tn), lambda i, j: (0, j + nb)),
        ],
        out_specs=pl.BlockSpec((tm, tn), lambda i, j: (i, j)),
        scratch_shapes=[pltpu.VMEM((tm, d), BF16)],
        compiler_params=_cparams(("parallel", "arbitrary")),
        name="norm_glu",
    )(h, nw, w, w, b, b)


_CONV_ROWS = 64
_CONV_LANES = 256


def _conv_module_kernel(uc_ref, up_ref, un_ref, h_ref, wdw_ref, bdw_ref, g_ref, b_ref, w2_ref, b2_ref,
                        o_ref, ext_ref, cv_ref, sh_ref):
    i = pl.program_id(1)
    tm = uc_ref.shape[1]
    d = uc_ref.shape[2]
    pad = (CONV_KERNEL - 1) // 2
    ext_ref[0:CONV_HALO, :] = jnp.where(i > 0, up_ref[0], 0.0)
    ext_ref[CONV_HALO:CONV_HALO + tm, :] = uc_ref[0]
    ext_ref[CONV_HALO + tm:, :] = jnp.where(i < pl.num_programs(1) - 1, un_ref[0], 0.0)

    base = CONV_HALO - pad
    span = sh_ref.shape[1]

    def lane_block(cb, carry):
        cols = pl.ds(pl.multiple_of(cb * _CONV_LANES, _CONV_LANES), _CONV_LANES)
        for s in range(1, SUBLANES):
            sh_ref[s - 1] = ext_ref[s:s + span, cols]
        for r0 in range(0, tm, _CONV_ROWS):
            acc = jnp.zeros((_CONV_ROWS, _CONV_LANES), F32)
            for j in range(CONV_KERNEL):
                s = (base + j) % SUBLANES
                a = r0 + base + j - s
                rows = ext_ref[a:a + _CONV_ROWS, cols] if s == 0 else sh_ref[s - 1, a:a + _CONV_ROWS, :]
                acc = acc + wdw_ref[j:j + 1, cols] * rows
            cv_ref[r0:r0 + _CONV_ROWS, cols] = acc + bdw_ref[:, cols]
        return carry

    lax.fori_loop(0, d // _CONV_LANES, lane_block, 0)

    x = cv_ref[...]
    mu = jnp.mean(x, axis=-1, keepdims=True)
    xc = x - mu
    var = jnp.mean(xc * xc, axis=-1, keepdims=True)
    y = xc * lax.rsqrt(var + LN_EPS) * g_ref[...] + b_ref[...]
    y = y * _sigmoid(y)
    o_ref[0] = h_ref[0] + jnp.dot(y.astype(BF16), w2_ref[...], preferred_element_type=F32) + b2_ref[...]


def _conv_module(u, h, wdw, bdw, ln_g, ln_b, w2, b2, *, tm):
    bsz, s, d = u.shape
    nh = tm // CONV_HALO
    last = s // CONV_HALO - 1
    tile = pl.BlockSpec((1, tm, d), lambda b, i: (b, i, 0))
    vec = pl.BlockSpec((1, d), lambda b, i: (0, 0))
    return pl.pallas_call(
        _conv_module_kernel,
        out_shape=jax.ShapeDtypeStruct((bsz, s, d), F32),
        grid=(bsz, s // tm),
        in_specs=[
            tile,
            pl.BlockSpec((1, CONV_HALO, d), lambda b, i: (b, jnp.maximum(i * nh - 1, 0), 0)),
            pl.BlockSpec((1, CONV_HALO, d), lambda b, i: (b, jnp.minimum((i + 1) * nh, last), 0)),
            tile,
            pl.BlockSpec((CONV_KERNEL, d), lambda b, i: (0, 0)),
            vec, vec, vec,
            pl.BlockSpec((d, d), lambda b, i: (0, 0), pipeline_mode=pl.Buffered(1)),
            vec,
        ],
        out_specs=tile,
        scratch_shapes=[pltpu.VMEM((tm + 2 * CONV_HALO, d), F32), pltpu.VMEM((tm, d), F32),
                        pltpu.VMEM((SUBLANES - 1, tm + 2 * CONV_HALO - SUBLANES, _CONV_LANES), F32)],
        compiler_params=_cparams(("parallel", "parallel")),
        name="conv_module",
    )(u, u, u, h, wdw, bdw, ln_g, ln_b, w2, b2)


def _ffn_kernel(h_ref, nw_ref, wg_ref, wu_ref, wd_ref, fw_ref, o_ref, hn_ref, acc_ref, *, final_norm):
    j = pl.program_id(1)

    @pl.when(j == 0)
    def _():
        hn_ref[...] = _rms(h_ref[...], nw_ref[...]).astype(BF16)
        acc_ref[...] = jnp.zeros_like(acc_ref)

    hn = hn_ref[...]
    g = jnp.dot(hn, wg_ref[...], preferred_element_type=F32)
    u = jnp.dot(hn, wu_ref[...], preferred_element_type=F32)
    a = (g * _sigmoid(g) * u).astype(BF16)
    acc_ref[...] += jnp.dot(a, wd_ref[...], preferred_element_type=F32)

    @pl.when(j == pl.num_programs(1) - 1)
    def _():
        y = h_ref[...] + acc_ref[...]
        if final_norm:
            y = _rms(y, fw_ref[...])
        o_ref[...] = y


def _ffn(h, nw, w_gate_up, w_down, fw, *, tm, th, final_norm):
    m, d = h.shape
    hid = w_down.shape[0]
    nb = hid // th
    vec = pl.BlockSpec((1, d), lambda i, j: (0, 0))
    return pl.pallas_call(
        functools.partial(_ffn_kernel, final_norm=final_norm),
        out_shape=jax.ShapeDtypeStruct((m, d), F32),
        grid=(m // tm, nb),
        in_specs=[
            pl.BlockSpec((tm, d), lambda i, j: (i, 0)),
            vec,
            pl.BlockSpec((d, th), lambda i, j: (0, j)),
            pl.BlockSpec((d, th), lambda i, j: (0, j + nb)),
            pl.BlockSpec((th, d), lambda i, j: (j, 0)),
            vec,
        ],
        out_specs=pl.BlockSpec((tm, d), lambda i, j: (i, 0)),
        scratch_shapes=[pltpu.VMEM((tm, d), BF16), pltpu.VMEM((tm, d), F32)],
        compiler_params=_cparams(("parallel", "arbitrary")),
        name="ffn",
    )(h, nw, w_gate_up, w_gate_up, w_down, fw)


def _dn_prep_kernel(xc_ref, xp_ref, xn_ref, wc_ref, ba_ref, alog_ref, dtb_ref,
                    q_ref, k_ref, v_ref, gate_ref, ext_ref):
    i = pl.program_id(1)
    tm = xc_ref.shape[1]
    key_dim = q_ref.shape[2]
    pad = (SHORT_CONV - 1) // 2
    ext_ref[0:SHORT_HALO, :] = jnp.where(i > 0, xp_ref[0], 0.0)
    ext_ref[SHORT_HALO:SHORT_HALO + tm, :] = xc_ref[0]
    ext_ref[SHORT_HALO + tm:, :] = jnp.where(i < pl.num_programs(1) - 1, xn_ref[0], 0.0)

    base = SHORT_HALO - pad
    n_heads = xc_ref.shape[2] // HEAD_DIM
    for hd in range(n_heads):
        cols = slice(hd * HEAD_DIM, (hd + 1) * HEAD_DIM)
        acc = jnp.zeros((tm, HEAD_DIM), F32)
        for j in range(SHORT_CONV):
            acc = acc + wc_ref[j:j + 1, cols] * ext_ref[base + j:base + j + tm, cols]
        y = acc * _sigmoid(acc)
        c0 = hd * HEAD_DIM
        if c0 < 2 * key_dim:
            y = y * lax.rsqrt(jnp.sum(y * y, axis=-1, keepdims=True) + L2_EPS)
            if c0 < key_dim:
                q_ref[0, :, cols] = (y * (HEAD_DIM ** -0.5)).astype(BF16)
            else:
                k_ref[0, :, c0 - key_dim:c0 - key_dim + HEAD_DIM] = y.astype(BF16)
        else:
            v_ref[0, :, c0 - 2 * key_dim:c0 - 2 * key_dim + HEAD_DIM] = y.astype(BF16)

    raw = ba_ref[0]
    beta = _sigmoid(raw)
    z = raw + dtb_ref[...]
    softplus = jnp.maximum(z, 0.0) + jnp.log(1.0 + jnp.exp(-jnp.abs(z)))
    g = -jnp.exp(alog_ref[...]) * softplus
    r = lax.broadcasted_iota(jnp.int32, (2 * CHUNK, CHUNK), 0)
    c = lax.broadcasted_iota(jnp.int32, (2 * CHUNK, CHUNK), 1)
    tri = ((r < CHUNK) & (c <= r)) | ((r >= CHUNK) & (c >= r - CHUNK))
    tri = jnp.where(tri, 1.0, 0.0).astype(BF16)
    g_hi = g.astype(BF16)
    rest = g - g_hi.astype(F32)
    g_mid = rest.astype(BF16)
    g_lo = (rest - g_mid.astype(F32)).astype(BF16)
    quarter = HEAD_DIM // 4
    lane = lax.broadcasted_iota(jnp.int32, (CHUNK, HEAD_DIM), 1)
    for n in range(tm // CHUNK):
        rows = slice(n * CHUNK, (n + 1) * CHUNK)
        sums = (jnp.dot(tri, g_hi[rows], preferred_element_type=F32)
                + jnp.dot(tri, g_mid[rows], preferred_element_type=F32)
                + jnp.dot(tri, g_lo[rows], preferred_element_type=F32))
        gate_ref[0, rows, :] = jnp.where(lane < 2 * quarter, beta[rows],
                                         jnp.where(lane < 3 * quarter, sums[:CHUNK], sums[CHUNK:]))


def _dn_prep(proj, ba, w_conv, alog_pad, dtb_pad, *, tm, key_dim, value_dim):
    bsz, s, _ = proj.shape
    qkv_dim = 2 * key_dim + value_dim
    nh = tm // SHORT_HALO
    last = s // SHORT_HALO - 1
    return pl.pallas_call(
        _dn_prep_kernel,
        out_shape=[
            jax.ShapeDtypeStruct((bsz, s, key_dim), BF16),
            jax.ShapeDtypeStruct((bsz, s, key_dim), BF16),
            jax.ShapeDtypeStruct((bsz, s, value_dim), BF16),
            jax.ShapeDtypeStruct((bsz, s, HEAD_DIM), F32),
        ],
        grid=(bsz, s // tm),
        in_specs=[
            pl.BlockSpec((1, tm, qkv_dim), lambda b, i: (b, i, 0)),
            pl.BlockSpec((1, SHORT_HALO, qkv_dim), lambda b, i: (b, jnp.maximum(i * nh - 1, 0), 0)),
            pl.BlockSpec((1, SHORT_HALO, qkv_dim), lambda b, i: (b, jnp.minimum((i + 1) * nh, last), 0)),
            pl.BlockSpec((SHORT_CONV, qkv_dim), lambda b, i: (0, 0)),
            pl.BlockSpec((1, tm, HEAD_DIM), lambda b, i: (b, i, 0)),
            pl.BlockSpec((1, HEAD_DIM), lambda b, i: (0, 0)),
            pl.BlockSpec((1, HEAD_DIM), lambda b, i: (0, 0)),
        ],
        out_specs=[
            pl.BlockSpec((1, tm, key_dim), lambda b, i: (b, i, 0)),
            pl.BlockSpec((1, tm, key_dim), lambda b, i: (b, i, 0)),
            pl.BlockSpec((1, tm, value_dim), lambda b, i: (b, i, 0)),
            pl.BlockSpec((1, tm, HEAD_DIM), lambda b, i: (b, i, 0)),
        ],
        scratch_shapes=[pltpu.VMEM((tm + 2 * SHORT_HALO, qkv_dim), F32)],
        compiler_params=_cparams(("parallel", "parallel")),
        name="dn_prep",
    )(proj, proj, proj, w_conv, ba, alog_pad, dtb_pad)


N_CHAINS = 2 * V_PER_K
WIDE = N_CHAINS * CHUNK
PAIR = V_PER_K * HEAD_DIM


def _block_diag(x):
    n = x.shape[1]
    rb = lax.broadcasted_iota(jnp.int32, (n, n), 0) // CHUNK
    cb = lax.broadcasted_iota(jnp.int32, (n, n), 1) // CHUNK
    tiled = jnp.concatenate([x.astype(BF16)] * N_CHAINS, axis=0)
    return jnp.where(rb == cb, tiled, jnp.zeros_like(tiled))


def _delta_intra_kernel(q_ref, k_ref, v_ref, g_ref, grow_ref, wq_ref, u_ref, ak_ref, dw_ref):
    kh = pl.program_id(1)
    cs = CHUNK
    quarter = HEAD_DIM // 4
    chunks = range(q_ref.shape[1] // cs)

    r = lax.broadcasted_iota(jnp.int32, (cs, WIDE), 0)
    lw = lax.broadcasted_iota(jnp.int32, (cs, WIDE), 1)
    blk = lw // cs
    c = lw - blk * cs
    ahead = jnp.where(blk < V_PER_K, r - c, c - r)
    incl = ahead >= 0
    strict = ahead > 0
    eye = jnp.where(r == c, 1.0, 0.0)
    lane_g = lax.broadcasted_iota(jnp.int32, (cs, HEAD_DIM), 1)

    def wide(cols):
        out = cols[N_CHAINS - 1]
        for j in range(N_CHAINS - 2, -1, -1):
            out = jnp.where(blk == j, cols[j], out)
        return out

    q, k, v, grow, beta_c, g_c, decay, p, t, a_w = ([None] * len(chunks) for _ in range(10))
    for n in chunks:
        rows = slice(n * cs, (n + 1) * cs)
        q[n], k[n], v[n], grow[n] = q_ref[0, rows, :], k_ref[0, rows, :], v_ref[0, rows, :], grow_ref[0, 0, n]
        gates = g_ref[0, rows, :]
        beta_c[n], g_c[n] = [], []
        for j in range(N_CHAINS):
            d, i = divmod(j, V_PER_K)
            head = kh * V_PER_K + i
            beta_c[n].append(jnp.sum(jnp.where(lane_g == d * quarter + head, gates, 0.0), axis=1, keepdims=True))
            g_c[n].append(jnp.sum(jnp.where(lane_g == (2 + d) * quarter + head, gates, 0.0), axis=1,
                                  keepdims=True))
        decay[n] = jnp.where(incl, jnp.exp(jnp.where(incl, wide(g_c[n]) - grow[n], 0.0)), 0.0)
        qkk = _mm_nt(jnp.concatenate([q[n], k[n]], axis=0), jnp.concatenate([k[n]] * N_CHAINS, axis=0))
        p[n] = -jnp.where(strict, qkk[cs:] * decay[n] * wide(beta_c[n]), 0.0)
        t[n] = eye + p[n]
        a_w[n] = jnp.where(incl, qkk[:cs] * decay[n], 0.0)

    for n in chunks:
        p[n] = _mm(p[n], _block_diag(p[n]))
    for _ in range(cs.bit_length() - 3):
        for n in chunks:
            both = _mm(jnp.concatenate([p[n], t[n]], axis=0), _block_diag(p[n]))
            p[n] = both[:cs]
            t[n] = t[n] + both[cs:]
    for n in chunks:
        t[n] = t[n] + _mm(t[n], _block_diag(p[n]))

    for n in chunks:
        kf = k[n].astype(F32)
        qf = q[n].astype(F32)
        rhs, qg, kg, dwl = [], [], [], []
        for j in range(N_CHAINS):
            d, i = divmod(j, V_PER_K)
            eg = jnp.exp(g_c[n][j])
            vj = v[n][:, i * HEAD_DIM:(i + 1) * HEAD_DIM].astype(F32)
            rhs.append(jnp.concatenate([vj * beta_c[n][j], kf * (beta_c[n][j] * eg)], axis=1).astype(BF16))
            g_last = grow[n][:, j * cs:j * cs + 1] if d == 1 else grow[n][:, (j + 1) * cs - 1:(j + 1) * cs]
            qg.append(qf * eg)
            kg.append(kf * jnp.exp(g_last - g_c[n][j]))
            dwl.append(jnp.broadcast_to(jnp.exp(g_last), (1, HEAD_DIM)))
        lhs = jnp.concatenate([jnp.where(blk == j, t[n], 0.0) for j in range(N_CHAINS)], axis=0)
        uw = _mm(lhs, jnp.concatenate(rhs, axis=0))
        for d in range(2):
            j0, j1 = d * V_PER_K, d * V_PER_K + 1
            u0, w0 = uw[j0 * cs:(j0 + 1) * cs, :HEAD_DIM], uw[j0 * cs:(j0 + 1) * cs, HEAD_DIM:]
            u1, w1 = uw[j1 * cs:(j1 + 1) * cs, :HEAD_DIM], uw[j1 * cs:(j1 + 1) * cs, HEAD_DIM:]
            wq = jnp.concatenate([jnp.concatenate([w0, w1], axis=1),
                                  jnp.concatenate([qg[j0], qg[j1]], axis=1)], axis=0)
            kgt = jnp.concatenate([kg[j0], kg[j1]], axis=0).T
            ak = jnp.concatenate([a_w[n][:, d * 2 * cs:(d + 1) * 2 * cs], kgt], axis=0)
            wq_ref[d, 0, 0, n] = wq.astype(BF16)
            u_ref[d, 0, 0, n] = jnp.concatenate([u0, u1], axis=1).astype(BF16)
            ak_ref[d, 0, 0, n] = ak.astype(BF16)
            dw_ref[d, 0, 0, n] = jnp.concatenate([dwl[j0], dwl[j1]], axis=1)


def _delta_intra(q, k, v, gates, grow, *, chunks_per_step):
    bsz, s, key_dim = q.shape
    nkh = key_dim // HEAD_DIM
    nc = s // CHUNK
    tc = chunks_per_step
    tm = tc * CHUNK
    tok = lambda w: pl.BlockSpec((1, tm, w), lambda b, h, n: (b, n, h))
    out = lambda rows, cols: pl.BlockSpec((2, 1, 1, tc, rows, cols), lambda b, h, n: (0, b, h, n, 0, 0))
    shape = lambda rows, cols, dt: jax.ShapeDtypeStruct((2, bsz, nkh, nc, rows, cols), dt)
    return pl.pallas_call(
        _delta_intra_kernel,
        out_shape=[shape(2 * CHUNK, PAIR, BF16), shape(CHUNK, PAIR, BF16),
                   shape(CHUNK + HEAD_DIM, 2 * CHUNK, BF16), shape(1, PAIR, F32)],
        grid=(bsz, nkh, nc // tc),
        in_specs=[tok(HEAD_DIM), tok(HEAD_DIM), tok(PAIR),
                  pl.BlockSpec((1, tm, HEAD_DIM), lambda b, h, n: (b, n, 0)),
                  pl.BlockSpec((1, 1, tc, 1, WIDE), lambda b, h, n: (b, h, n, 0, 0))],
        out_specs=[out(2 * CHUNK, PAIR), out(CHUNK, PAIR), out(CHUNK + HEAD_DIM, 2 * CHUNK), out(1, PAIR)],
        compiler_params=_cparams(("parallel", "parallel", "parallel")),
        name="delta_intra",
    )(q, k, v, gates, grow)


def _delta_state_kernel(wqf_ref, uf_ref, akf_ref, dwf_ref, wqb_ref, ub_ref, akb_ref, dwb_ref,
                        of_ref, ob_ref, state_ref):
    @pl.when(pl.program_id(2) == 0)
    def _():
        state_ref[...] = jnp.zeros_like(state_ref)

    hb = wqf_ref.shape[2]
    zeros_s = jnp.zeros((HEAD_DIM, HEAD_DIM), BF16)
    zeros_v = jnp.zeros((CHUNK, HEAD_DIM), BF16)
    work = []
    for d, (wq_ref, u_ref, ak_ref, dw_ref) in enumerate(
            ((wqf_ref, uf_ref, akf_ref, dwf_ref), (wqb_ref, ub_ref, akb_ref, dwb_ref))):
        for h in range(hb):
            work.append((d, h, wq_ref[0, 0, h, 0], u_ref[0, 0, h, 0], ak_ref[0, 0, h, 0], dw_ref[0, 0, h, 0],
                         state_ref[d * hb + h]))
    ws_all = []
    for d, h, wq, u, ak, dw, state in work:
        sb = state.astype(BF16)
        s_bd = jnp.concatenate([jnp.concatenate([sb[:, :HEAD_DIM], zeros_s], axis=1),
                                jnp.concatenate([zeros_s, sb[:, HEAD_DIM:]], axis=1)], axis=0)
        ws_all.append(jnp.dot(wq, s_bd, preferred_element_type=F32))
    akv_all = []
    for (d, h, wq, u, ak, dw, state), ws in zip(work, ws_all):
        v_new = (u.astype(F32) - ws[:CHUNK]).astype(BF16)
        v_bd = jnp.concatenate([jnp.concatenate([v_new[:, :HEAD_DIM], zeros_v], axis=1),
                                jnp.concatenate([zeros_v, v_new[:, HEAD_DIM:]], axis=1)], axis=0)
        akv_all.append(jnp.dot(ak, v_bd, preferred_element_type=F32))
    for (d, h, wq, u, ak, dw, state), ws, akv in zip(work, ws_all, akv_all):
        o_ref = ob_ref if d else of_ref
        o_ref[0, :, h * PAIR:(h + 1) * PAIR] = ws[CHUNK:] + akv[:CHUNK]
        state_ref[d * hb + h] = state * dw + akv[CHUNK:]


def _delta_state(wq, u, ak, dw, *, heads_per_step):
    _, bsz, nkh, nc = wq.shape[:4]
    hb = heads_per_step
    s = nc * CHUNK
    fwd = lambda rows, cols: pl.BlockSpec((1, 1, hb, 1, rows, cols), lambda b, h, c: (0, b, h, c, 0, 0))
    bwd = lambda rows, cols: pl.BlockSpec((1, 1, hb, 1, rows, cols), lambda b, h, c: (1, b, h, nc - 1 - c, 0, 0))
    shapes = ((2 * CHUNK, PAIR), (CHUNK, PAIR), (CHUNK + HEAD_DIM, 2 * CHUNK), (1, PAIR))
    return pl.pallas_call(
        _delta_state_kernel,
        out_shape=[jax.ShapeDtypeStruct((bsz, s, nkh * PAIR), F32)] * 2,
        grid=(bsz, nkh // hb, nc),
        in_specs=[fwd(*sh) for sh in shapes] + [bwd(*sh) for sh in shapes],
        out_specs=[pl.BlockSpec((1, CHUNK, hb * PAIR), lambda b, h, c: (b, c, h)),
                   pl.BlockSpec((1, CHUNK, hb * PAIR), lambda b, h, c: (b, nc - 1 - c, h))],
        scratch_shapes=[pltpu.VMEM((2 * hb, HEAD_DIM, PAIR), F32)],
        compiler_params=_cparams(("parallel", "parallel", "arbitrary")),
        name="delta_state",
    )(wq, u, ak, dw, wq, u, ak, dw)


def _delta_rule(q, k, v, gates):
    bsz, s, key_dim = q.shape
    nkh = key_dim // HEAD_DIM
    nc = s // CHUNK
    g = gates[:, :, HEAD_DIM // 2:].reshape(bsz, nc, CHUNK, 2, nkh, V_PER_K)
    grow = g.transpose(0, 4, 1, 3, 5, 2).reshape(bsz, nkh, nc, 1, WIDE)
    wq, u, ak, dw = _delta_intra(q, k, v, gates, grow, chunks_per_step=8)
    return _delta_state(wq, u, ak, dw, heads_per_step=4)


def _dn_out_kernel(of_ref, ob_ref, z_ref, nw_ref, w_ref, h_ref, o_ref, y_ref):
    for hd in range(of_ref.shape[1] // HEAD_DIM):
        cols = slice(hd * HEAD_DIM, (hd + 1) * HEAD_DIM)
        o = of_ref[:, cols] + ob_ref[:, cols]
        z = z_ref[:, cols]
        y = _rms(o, nw_ref[...]) * (z * _sigmoid(z))
        y_ref[:, cols] = y.astype(BF16)
    o_ref[...] = h_ref[...] + jnp.dot(y_ref[...], w_ref[...], preferred_element_type=F32)


def _dn_out(o_f, o_b, proj, z_block, norm_w, w_out, h, *, tm):
    m, vd = o_f.shape
    d = h.shape[1]
    return pl.pallas_call(
        _dn_out_kernel,
        out_shape=jax.ShapeDtypeStruct((m, d), F32),
        grid=(m // tm,),
        in_specs=[
            pl.BlockSpec((tm, vd), lambda i: (i, 0)),
            pl.BlockSpec((tm, vd), lambda i: (i, 0)),
            pl.BlockSpec((tm, vd), lambda i: (i, z_block)),
            pl.BlockSpec((1, HEAD_DIM), lambda i: (0, 0)),
            pl.BlockSpec((vd, d), lambda i: (0, 0), pipeline_mode=pl.Buffered(1)),
            pl.BlockSpec((tm, d), lambda i: (i, 0)),
        ],
        out_specs=pl.BlockSpec((tm, d), lambda i: (i, 0)),
        scratch_shapes=[pltpu.VMEM((tm, vd), BF16)],
        compiler_params=_cparams(("parallel",)),
        name="dn_out",
    )(o_f, o_b, proj, norm_w, w_out, h)


def _row(v):
    return v.reshape(1, -1).astype(F32)


def _conformer_layer(h, nw, w_pw1, b_pw1, w_dw, b_dw, ln_g, ln_b, w_pw2, b_pw2):
    bsz, s, d = h.shape
    u = _norm_glu(h.reshape(bsz * s, d), _row(nw), w_pw1.astype(BF16), _row(b_pw1), tm=1024, tn=512)
    return _conv_module(u.reshape(bsz, s, d), h, w_dw, _row(b_dw), _row(ln_g), _row(ln_b),
                        w_pw2.astype(BF16), _row(b_pw2), tm=256)


def _deltanet_layer(h, nw, w_in, w_conv, a_log, dt_bias, norm_w, w_out):
    bsz, s, d = h.shape
    key_dim = NUM_K_HEADS * HEAD_DIM
    value_dim = NUM_V_HEADS * HEAD_DIM
    main = 2 * key_dim + value_dim + value_dim
    h2 = h.reshape(bsz * s, d)
    w_in = w_in.astype(BF16)
    proj = _norm_mm(h2, _row(nw), w_in[:, :main], jnp.zeros((1, main), F32), tm=1024, tn=1024)
    ba = _norm_mm(h2, _row(nw), w_in[:, main:], jnp.zeros((1, w_in.shape[1] - main), F32), tm=512,
                  tn=w_in.shape[1] - main)
    zeros = jnp.zeros((2 * NUM_V_HEADS,), F32)
    alog_pad = jnp.concatenate([zeros, a_log.reshape(-1).astype(F32)]).reshape(1, -1)
    dtb_pad = jnp.concatenate([zeros, dt_bias.reshape(-1).astype(F32)]).reshape(1, -1)
    q, k, v, gates = _dn_prep(proj.reshape(bsz, s, main), ba.reshape(bsz, s, -1), w_conv, alog_pad, dtb_pad,
                              tm=256, key_dim=key_dim, value_dim=value_dim)
    o_f, o_b = _delta_rule(q, k, v, gates)
    out = _dn_out(o_f.reshape(bsz * s, value_dim), o_b.reshape(bsz * s, value_dim), proj,
                  (2 * key_dim + value_dim) // value_dim, _row(norm_w), w_out.astype(BF16), h2, tm=256)
    return out.reshape(bsz, s, d)


def kernel(x, mix_norm, ffn_norm, final_norm, cv_w_pw1, cv_b_pw1, cv_w_dw, cv_b_dw, cv_ln_g, cv_ln_b,
           cv_w_pw2, cv_b_pw2, dn_w_in, dn_w_conv, dn_a_log, dn_dt_bias, dn_norm_w, dn_w_out,
           ffn_w_gate_up, ffn_w_down):
    bsz, s, d = x.shape
    depth = mix_norm.shape[0]
    h = x
    for i in range(depth):
        j = i // 2
        if i % 2 == 0:
            h = _conformer_layer(h, mix_norm[i], cv_w_pw1[j], cv_b_pw1[j], cv_w_dw[j], cv_b_dw[j],
                                 cv_ln_g[j], cv_ln_b[j], cv_w_pw2[j], cv_b_pw2[j])
        else:
            h = _deltanet_layer(h, mix_norm[i], dn_w_in[j], dn_w_conv[j], dn_a_log[j], dn_dt_bias[j],
                                dn_norm_w[j], dn_w_out[j])
        last = i == depth - 1
        h = _ffn(h.reshape(bsz * s, d), _row(ffn_norm[i]), ffn_w_gate_up[i].astype(BF16),
                 ffn_w_down[i].astype(BF16), _row(final_norm), tm=512, th=512,
                 final_norm=last).reshape(bsz, s, d)
    return h
```

```python
import functools

import jax
import jax.numpy as jnp
from jax import lax
from jax.experimental import pallas as pl
from jax.experimental.pallas import tpu as pltpu

F32 = jnp.float32
BF16 = jnp.bfloat16

RMS_EPS = 1e-6
LN_EPS = 1e-5
L2_EPS = 1e-6

CONV_KERNEL = 31
CONV_HALO = 16
SHORT_CONV = 5
SHORT_HALO = 16

SUBLANES = 8
HEAD_DIM = 128
NUM_K_HEADS = 16
NUM_V_HEADS = 32
V_PER_K = NUM_V_HEADS // NUM_K_HEADS
CHUNK = 64

VMEM_LIMIT = 56 * 1024 * 1024


def _cparams(sem):
    return pltpu.CompilerParams(dimension_semantics=sem, vmem_limit_bytes=VMEM_LIMIT)


def _mm(a, b):
    return jnp.dot(a.astype(BF16), b.astype(BF16), preferred_element_type=F32)


def _mm_nt(a, b):
    return lax.dot_general(a.astype(BF16), b.astype(BF16), (((1,), (1,)), ((), ())),
                           preferred_element_type=F32)


def _rms(x, w):
    ms = jnp.mean(x * x, axis=-1, keepdims=True)
    return x * lax.rsqrt(ms + RMS_EPS) * w


def _sigmoid(x):
    return 1.0 / (1.0 + jnp.exp(-x))


def _norm_glu_kernel(h_ref, nw_ref, wa_ref, wg_ref, ba_ref, bg_ref, o_ref, hn_ref):
    @pl.when(pl.program_id(1) == 0)
    def _():
        hn_ref[...] = _rms(h_ref[...], nw_ref[...]).astype(BF16)

    hn = hn_ref[...]
    a = jnp.dot(hn, wa_ref[...], preferred_element_type=F32) + ba_ref[...]
    g = jnp.dot(hn, wg_ref[...], preferred_element_type=F32) + bg_ref[...]
    o_ref[...] = a * _sigmoid(g)


def _norm_glu(h, nw, w, b, *, tm, tn):
    m, d = h.shape
    n = w.shape[1] // 2
    nb = n // tn
    return pl.pallas_call(
        _norm_glu_kernel,
        out_shape=jax.ShapeDtypeStruct((m, n), F32),
        grid=(m // tm, nb),
        in_specs=[
            pl.BlockSpec((tm, d), lambda i, j: (i, 0)),
            pl.BlockSpec((1, d), lambda i, j: (0, 0)),
            pl.BlockSpec((d, tn), lambda i, j: (0, j)),
            pl.BlockSpec((d, tn), lambda i, j: (0, j + nb)),
            pl.BlockSpec((1, tn), lambda i, j: (0, j)),
            pl.BlockSpec((1, tn), lambda i, j: (0, j + nb)),
        ],
        out_specs=pl.BlockSpec((tm, tn), lambda i, j: (i, j)),
        scratch_shapes=[pltpu.VMEM((tm, d), BF16)],
        compiler_params=_cparams(("parallel", "arbitrary")),
        name="norm_glu",
    )(h, nw, w, w, b, b)


_CONV_ROWS = 64
_CONV_LANES = 256


def _conv_module_kernel(uc_ref, up_ref, un_ref, h_ref, wdw_ref, bdw_ref, g_ref, b_ref, w2_ref, b2_ref,
                        o_ref, ext_ref, cv_ref, sh_ref):
    i = pl.program_id(1)
    tm = uc_ref.shape[1]
    d = uc_ref.shape[2]
    pad = (CONV_KERNEL - 1) // 2
    ext_ref[0:CONV_HALO, :] = jnp.where(i > 0, up_ref[0], 0.0)
    ext_ref[CONV_HALO:CONV_HALO + tm, :] = uc_ref[0]
    ext_ref[CONV_HALO + tm:, :] = jnp.where(i < pl.num_programs(1) - 1, un_ref[0], 0.0)

    base = CONV_HALO - pad
    span = sh_ref.shape[1]

    def lane_block(cb, carry):
        cols = pl.ds(pl.multiple_of(cb * _CONV_LANES, _CONV_LANES), _CONV_LANES)
        for s in range(1, SUBLANES):
            sh_ref[s - 1] = ext_ref[s:s + span, cols]
        for r0 in range(0, tm, _CONV_ROWS):
            acc = jnp.zeros((_CONV_ROWS, _CONV_LANES), F32)
            for j in range(CONV_KERNEL):
                s = (base + j) % SUBLANES
                a = r0 + base + j - s
                rows = ext_ref[a:a + _CONV_ROWS, cols] if s == 0 else sh_ref[s - 1, a:a + _CONV_ROWS, :]
                acc = acc + wdw_ref[j:j + 1, cols] * rows
            cv_ref[r0:r0 + _CONV_ROWS, cols] = acc + bdw_ref[:, cols]
        return carry

    lax.fori_loop(0, d // _CONV_LANES, lane_block, 0)

    x = cv_ref[...]
    mu = jnp.mean(x, axis=-1, keepdims=True)
    xc = x - mu
    var = jnp.mean(xc * xc, axis=-1, keepdims=True)
    y = xc * lax.rsqrt(var + LN_EPS) * g_ref[...] + b_ref[...]
    y = y * _sigmoid(y)
    o_ref[0] = h_ref[0] + jnp.dot(y.astype(BF16), w2_ref[...], preferred_element_type=F32) + b2_ref[...]


def _conv_module(u, h, wdw, bdw, ln_g, ln_b, w2, b2, *, tm):
    bsz, s, d = u.shape
    nh = tm // CONV_HALO
    last = s // CONV_HALO - 1
    tile = pl.BlockSpec((1, tm, d), lambda b, i: (b, i, 0))
    vec = pl.BlockSpec((1, d), lambda b, i: (0, 0))
    return pl.pallas_call(
        _conv_module_kernel,
        out_shape=jax.ShapeDtypeStruct((bsz, s, d), F32),
        grid=(bsz, s // tm),
        in_specs=[
            tile,
            pl.BlockSpec((1, CONV_HALO, d), lambda b, i: (b, jnp.maximum(i * nh - 1, 0), 0)),
            pl.BlockSpec((1, CONV_HALO, d), lambda b, i: (b, jnp.minimum((i + 1) * nh, last), 0)),
            tile,
            pl.BlockSpec((CONV_KERNEL, d), lambda b, i: (0, 0)),
            vec, vec, vec,
            pl.BlockSpec((d, d), lambda b, i: (0, 0), pipeline_mode=pl.Buffered(1)),
            vec,
        ],
        out_specs=tile,
        scratch_shapes=[pltpu.VMEM((tm + 2 * CONV_HALO, d), F32), pltpu.VMEM((tm, d), F32),
                        pltpu.VMEM((SUBLANES - 1, tm + 2 * CONV_HALO - SUBLANES, _CONV_LANES), F32)],
        compiler_params=_cparams(("parallel", "parallel")),
        name="conv_module",
    )(u, u, u, h, wdw, bdw, ln_g, ln_b, w2, b2)


def _ffn_kernel(h_ref, nw_ref, wg_ref, wu_ref, wd_ref, fw_ref, o_ref, hn_ref, acc_ref, *, final_norm):
    j = pl.program_id(1)

    @pl.when(j == 0)
    def _():
        hn_ref[...] = _rms(h_ref[...], nw_ref[...]).astype(BF16)
        acc_ref[...] = jnp.zeros_like(acc_ref)

    hn = hn_ref[...]
    g = jnp.dot(hn, wg_ref[...], preferred_element_type=F32)
    u = jnp.dot(hn, wu_ref[...], preferred_element_type=F32)
    a = (g * _sigmoid(g) * u).astype(BF16)
    acc_ref[...] += jnp.dot(a, wd_ref[...], preferred_element_type=F32)

    @pl.when(j == pl.num_programs(1) - 1)
    def _():
        y = h_ref[...] + acc_ref[...]
        if final_norm:
            y = _rms(y, fw_ref[...])
        o_ref[...] = y


def _ffn(h, nw, w_gate_up, w_down, fw, *, tm, th, final_norm):
    m, d = h.shape
    hid = w_down.shape[0]
    nb = hid // th
    vec = pl.BlockSpec((1, d), lambda i, j: (0, 0))
    return pl.pallas_call(
        functools.partial(_ffn_kernel, final_norm=final_norm),
        out_shape=jax.ShapeDtypeStruct((m, d), F32),
        grid=(m // tm, nb),
        in_specs=[
            pl.BlockSpec((tm, d), lambda i, j: (i, 0)),
            vec,
            pl.BlockSpec((d, th), lambda i, j: (0, j)),
            pl.BlockSpec((d, th), lambda i, j: (0, j + nb)),
            pl.BlockSpec((th, d), lambda i, j: (j, 0)),
            vec,
        ],
        out_specs=pl.BlockSpec((tm, d), lambda i, j: (i, 0)),
        scratch_shapes=[pltpu.VMEM((tm, d), BF16), pltpu.VMEM((tm, d), F32)],
        compiler_params=_cparams(("parallel", "arbitrary")),
        name="ffn",
    )(h, nw, w_gate_up, w_gate_up, w_down, fw)


_MXU_LANES = 256
_PIECE_ROWS = 128


def _dn_in_kernel(hc_ref, hp_ref, hx_ref, nw_ref, w_ref, wt_ref, wc_ref, alog_ref, dtb_ref,
                  q_ref, k_ref, v_ref, z_ref, gate_ref, hn_ref, p_ref, *, n_q, n_k, n_v):
    i = pl.program_id(1)
    j = pl.program_id(2)
    tm = hc_ref.shape[1]
    tn = w_ref.shape[1]
    halo = SHORT_HALO
    pad = (SHORT_CONV - 1) // 2

    @pl.when(j == 0)
    def _():
        nw = nw_ref[...]
        hn_ref[0:halo, :] = jnp.where(i > 0, _rms(hp_ref[0], nw), 0.0).astype(BF16)
        hn_c = _rms(hc_ref[0], nw).astype(BF16)
        hn_ref[halo:halo + tm, :] = hn_c
        hn_ref[halo + tm:, :] = jnp.where(i < pl.num_programs(1) - 1, _rms(hx_ref[0], nw), 0.0).astype(BF16)
        _gates(jnp.dot(hn_c, wt_ref[...], preferred_element_type=F32), alog_ref, dtb_ref, gate_ref)

    n_sub = tn // _MXU_LANES

    n_piece = tm // _PIECE_ROWS
    bounds = [0] + [halo + _PIECE_ROWS * p for p in range(1, n_piece)] + [tm + 2 * halo]

    def project(t, p):
        rows = slice(bounds[p], bounds[p + 1])
        p_ref[t, rows, :] = jnp.dot(hn_ref[rows, :], w_ref[:, t * _MXU_LANES:(t + 1) * _MXU_LANES],
                                    preferred_element_type=F32)

    def conv_silu(t, r0):
        cols = slice(t * _MXU_LANES, (t + 1) * _MXU_LANES)
        acc = jnp.zeros((_CONV_ROWS, _MXU_LANES), F32)
        for tap in range(SHORT_CONV):
            a = r0 + halo - pad + tap
            acc = acc + wc_ref[tap:tap + 1, cols] * p_ref[t, a:a + _CONV_ROWS, :]
        return acc * _sigmoid(acc)

    def tile(epilogue):
        for p in range(n_piece):
            project(0, p)
        for t in range(n_sub):
            for p in range(n_piece):
                if t + 1 < n_sub:
                    project(t + 1, p)
                for r0 in range(p * _PIECE_ROWS, (p + 1) * _PIECE_ROWS, _CONV_ROWS):
                    epilogue(t, slice(r0, r0 + _CONV_ROWS), conv_silu(t, r0))

    def qk_epilogue(o_ref, scale):
        def epilogue(t, rows, y):
            for hd in range(_MXU_LANES // HEAD_DIM):
                yh = y[:, hd * HEAD_DIM:(hd + 1) * HEAD_DIM]
                yh = yh * (lax.rsqrt(jnp.sum(yh * yh, axis=-1, keepdims=True) + L2_EPS) * scale)
                c0 = t * _MXU_LANES + hd * HEAD_DIM
                o_ref[0, rows, c0:c0 + HEAD_DIM] = yh.astype(BF16)
        return epilogue

    def v_epilogue(t, rows, y):
        v_ref[0, rows, t * _MXU_LANES:(t + 1) * _MXU_LANES] = y.astype(BF16)

    @pl.when(j < n_q)
    def _():
        tile(qk_epilogue(q_ref, HEAD_DIM ** -0.5))

    @pl.when((j >= n_q) & (j < n_q + n_k))
    def _():
        tile(qk_epilogue(k_ref, 1.0))

    @pl.when((j >= n_q + n_k) & (j < n_q + n_k + n_v))
    def _():
        tile(v_epilogue)

    @pl.when(j >= n_q + n_k + n_v)
    def _():
        z_ref[0] = jnp.dot(hn_ref[halo:halo + tm, :], w_ref[...], preferred_element_type=F32)


def _gates(raw, alog_ref, dtb_ref, gate_ref):
    tm = raw.shape[0]
    beta = _sigmoid(raw)
    z = raw + dtb_ref[...]
    softplus = jnp.maximum(z, 0.0) + jnp.log(1.0 + jnp.exp(-jnp.abs(z)))
    g = -jnp.exp(alog_ref[...]) * softplus
    r = lax.broadcasted_iota(jnp.int32, (2 * CHUNK, CHUNK), 0)
    c = lax.broadcasted_iota(jnp.int32, (2 * CHUNK, CHUNK), 1)
    tri = ((r < CHUNK) & (c <= r)) | ((r >= CHUNK) & (c >= r - CHUNK))
    tri = jnp.where(tri, 1.0, 0.0).astype(BF16)
    g_hi = g.astype(BF16)
    rest = g - g_hi.astype(F32)
    g_mid = rest.astype(BF16)
    g_lo = (rest - g_mid.astype(F32)).astype(BF16)
    quarter = HEAD_DIM // 4
    lane = lax.broadcasted_iota(jnp.int32, (CHUNK, HEAD_DIM), 1)
    for n in range(tm // CHUNK):
        rows = slice(n * CHUNK, (n + 1) * CHUNK)
        sums = (jnp.dot(tri, g_hi[rows], preferred_element_type=F32)
                + jnp.dot(tri, g_mid[rows], preferred_element_type=F32)
                + jnp.dot(tri, g_lo[rows], preferred_element_type=F32))
        gate_ref[0, rows, :] = jnp.where(lane < 2 * quarter, beta[rows],
                                         jnp.where(lane < 3 * quarter, sums[:CHUNK], sums[CHUNK:]))


def _dn_in(h, nw, w_in, w_conv, alog_pad, dtb_pad, *, tm, tn, key_dim, value_dim):
    bsz, s, d = h.shape
    n_q = n_k = key_dim // tn
    n_v = n_z = value_dim // tn
    main = 2 * key_dim + 2 * value_dim
    tail = w_in.shape[1] - main
    nh = tm // SHORT_HALO
    last = s // SHORT_HALO - 1
    clip = lambda j, lo, n: jnp.clip(j - lo, 0, n - 1)
    col_tile = lambda lo, n: pl.BlockSpec((1, tm, tn), lambda b, i, j: (b, i, clip(j, lo, n)))
    vec = pl.BlockSpec((1, HEAD_DIM), lambda b, i, j: (0, 0))
    return pl.pallas_call(
        functools.partial(_dn_in_kernel, n_q=n_q, n_k=n_k, n_v=n_v),
        out_shape=[
            jax.ShapeDtypeStruct((bsz, s, key_dim), BF16),
            jax.ShapeDtypeStruct((bsz, s, key_dim), BF16),
            jax.ShapeDtypeStruct((bsz, s, value_dim), BF16),
            jax.ShapeDtypeStruct((bsz, s, value_dim), F32),
            jax.ShapeDtypeStruct((bsz, s, HEAD_DIM), F32),
        ],
        grid=(bsz, s // tm, n_q + n_k + n_v + n_z),
        in_specs=[
            pl.BlockSpec((1, tm, d), lambda b, i, j: (b, i, 0)),
            pl.BlockSpec((1, SHORT_HALO, d), lambda b, i, j: (b, jnp.maximum(i * nh - 1, 0), 0)),
            pl.BlockSpec((1, SHORT_HALO, d), lambda b, i, j: (b, jnp.minimum((i + 1) * nh, last), 0)),
            pl.BlockSpec((1, d), lambda b, i, j: (0, 0)),
            pl.BlockSpec((d, tn), lambda b, i, j: (0, j)),
            pl.BlockSpec((d, tail), lambda b, i, j: (0, main // tail)),
            pl.BlockSpec((SHORT_CONV, tn), lambda b, i, j: (0, jnp.minimum(j, n_q + n_k + n_v - 1))),
            vec, vec,
        ],
        out_specs=[col_tile(0, n_q), col_tile(n_q, n_k), col_tile(n_q + n_k, n_v),
                   col_tile(n_q + n_k + n_v, n_z),
                   pl.BlockSpec((1, tm, HEAD_DIM), lambda b, i, j: (b, i, 0))],
        scratch_shapes=[pltpu.VMEM((tm + 2 * SHORT_HALO, d), BF16),
                        pltpu.VMEM((tn // _MXU_LANES, tm + 2 * SHORT_HALO, _MXU_LANES), F32)],
        compiler_params=_cparams(("parallel", "parallel", "arbitrary")),
        name="dn_in",
    )(h, h, h, nw, w_in, w_in, w_conv, alog_pad, dtb_pad)


N_CHAINS = 2 * V_PER_K
WIDE = N_CHAINS * CHUNK
PAIR = V_PER_K * HEAD_DIM
OPS_U = 2 * CHUNK
OPS_AK = 3 * CHUNK
OPS_ROWS = OPS_AK + HEAD_DIM


def _block_diag(x):
    n = x.shape[1]
    rb = lax.broadcasted_iota(jnp.int32, (n, n), 0) // CHUNK
    cb = lax.broadcasted_iota(jnp.int32, (n, n), 1) // CHUNK
    tiled = jnp.concatenate([x.astype(BF16)] * N_CHAINS, axis=0)
    return jnp.where(rb == cb, tiled, jnp.zeros_like(tiled))


def _delta_intra_kernel(q_ref, k_ref, v_ref, g_ref, grow_ref, ops_ref, dw_ref):
    kh = pl.program_id(1)
    cs = CHUNK
    quarter = HEAD_DIM // 4
    chunks = range(q_ref.shape[1] // cs)

    r = lax.broadcasted_iota(jnp.int32, (cs, WIDE), 0)
    lw = lax.broadcasted_iota(jnp.int32, (cs, WIDE), 1)
    blk = lw // cs
    c = lw - blk * cs
    ahead = jnp.where(blk < V_PER_K, r - c, c - r)
    incl = ahead >= 0
    strict = ahead > 0
    eye = jnp.where(r == c, 1.0, 0.0)
    lane_g = lax.broadcasted_iota(jnp.int32, (cs, HEAD_DIM), 1)

    def wide(cols):
        out = cols[N_CHAINS - 1]
        for j in range(N_CHAINS - 2, -1, -1):
            out = jnp.where(blk == j, cols[j], out)
        return out

    q, k, v, grow, beta_c, g_c, decay, p, t, a_w = ([None] * len(chunks) for _ in range(10))
    for n in chunks:
        rows = slice(n * cs, (n + 1) * cs)
        q[n], k[n], v[n], grow[n] = q_ref[0, rows, :], k_ref[0, rows, :], v_ref[0, rows, :], grow_ref[0, 0, n]
        gates = g_ref[0, rows, :]
        beta_c[n], g_c[n] = [], []
        for j in range(N_CHAINS):
            d, i = divmod(j, V_PER_K)
            head = kh * V_PER_K + i
            beta_c[n].append(jnp.sum(jnp.where(lane_g == d * quarter + head, gates, 0.0), axis=1, keepdims=True))
            g_c[n].append(jnp.sum(jnp.where(lane_g == (2 + d) * quarter + head, gates, 0.0), axis=1,
                                  keepdims=True))
        decay[n] = jnp.where(incl, jnp.exp(jnp.where(incl, wide(g_c[n]) - grow[n], 0.0)), 0.0)
        qkk = _mm_nt(jnp.concatenate([q[n], k[n]], axis=0), jnp.concatenate([k[n]] * N_CHAINS, axis=0))
        p[n] = -jnp.where(strict, qkk[cs:] * decay[n] * wide(beta_c[n]), 0.0)
        t[n] = eye + p[n]
        a_w[n] = jnp.where(incl, qkk[:cs] * decay[n], 0.0)

    for n in chunks:
        p[n] = _mm(p[n], _block_diag(p[n]))
    for _ in range(cs.bit_length() - 3):
        for n in chunks:
            both = _mm(jnp.concatenate([p[n], t[n]], axis=0), _block_diag(p[n]))
            p[n] = both[:cs]
            t[n] = t[n] + both[cs:]
    for n in chunks:
        t[n] = t[n] + _mm(t[n], _block_diag(p[n]))

    for n in chunks:
        kf = k[n].astype(F32)
        qf = q[n].astype(F32)
        rhs, qg, kg, dwl = [], [], [], []
        for j in range(N_CHAINS):
            d, i = divmod(j, V_PER_K)
            eg = jnp.exp(g_c[n][j])
            vj = v[n][:, i * HEAD_DIM:(i + 1) * HEAD_DIM].astype(F32)
            rhs.append(jnp.concatenate([vj * beta_c[n][j], kf * (beta_c[n][j] * eg)], axis=1).astype(BF16))
            g_last = grow[n][:, j * cs:j * cs + 1] if d == 1 else grow[n][:, (j + 1) * cs - 1:(j + 1) * cs]
            qg.append(qf * eg)
            kg.append(kf * jnp.exp(g_last - g_c[n][j]))
            dwl.append(jnp.broadcast_to(jnp.exp(g_last), (1, HEAD_DIM)))
        lhs = jnp.concatenate([jnp.where(blk == j, t[n], 0.0) for j in range(N_CHAINS)], axis=0)
        uw = _mm(lhs, jnp.concatenate(rhs, axis=0))
        for d in range(2):
            j0, j1 = d * V_PER_K, d * V_PER_K + 1
            u0, w0 = uw[j0 * cs:(j0 + 1) * cs, :HEAD_DIM], uw[j0 * cs:(j0 + 1) * cs, HEAD_DIM:]
            u1, w1 = uw[j1 * cs:(j1 + 1) * cs, :HEAD_DIM], uw[j1 * cs:(j1 + 1) * cs, HEAD_DIM:]
            kgt = jnp.concatenate([kg[j0], kg[j1]], axis=0).T
            a_pad = jnp.concatenate([a_w[n][:, d * 2 * cs:(d + 1) * 2 * cs], jnp.zeros((HEAD_DIM - cs, 2 * cs), F32)],
                                    axis=0)
            ops = jnp.concatenate([jnp.concatenate([w0, w1], axis=1),
                                   jnp.concatenate([qg[j0], qg[j1]], axis=1),
                                   jnp.concatenate([u0, u1], axis=1),
                                   jnp.concatenate([a_pad, kgt], axis=1)], axis=0)
            ops_ref[d, 0, 0, n] = ops.astype(BF16)
            dw_ref[d, 0, 0, n:n + 1, :] = jnp.concatenate([dwl[j0], dwl[j1]], axis=1)


def _delta_intra(q, k, v, gates, grow, *, chunks_per_step):
    bsz, s, key_dim = q.shape
    nkh = key_dim // HEAD_DIM
    nc = s // CHUNK
    tc = chunks_per_step
    tm = tc * CHUNK
    tok = lambda w: pl.BlockSpec((1, tm, w), lambda b, h, n: (b, n, h))
    return pl.pallas_call(
        _delta_intra_kernel,
        out_shape=[jax.ShapeDtypeStruct((2, bsz, nkh, nc, OPS_ROWS, PAIR), BF16),
                   jax.ShapeDtypeStruct((2, bsz, nkh, nc, PAIR), F32)],
        grid=(bsz, nkh, nc // tc),
        in_specs=[tok(HEAD_DIM), tok(HEAD_DIM), tok(PAIR),
                  pl.BlockSpec((1, tm, HEAD_DIM), lambda b, h, n: (b, n, 0)),
                  pl.BlockSpec((1, 1, tc, 1, WIDE), lambda b, h, n: (b, h, n, 0, 0))],
        out_specs=[pl.BlockSpec((2, 1, 1, tc, OPS_ROWS, PAIR), lambda b, h, n: (0, b, h, n, 0, 0)),
                   pl.BlockSpec((2, 1, 1, tc, PAIR), lambda b, h, n: (0, b, h, n, 0))],
        compiler_params=_cparams(("parallel", "parallel", "parallel")),
        name="delta_intra",
    )(q, k, v, gates, grow)


def _delta_state_kernel(opsf_ref, dwf_ref, opsb_ref, dwb_ref, of_ref, ob_ref, state_ref):
    c = pl.program_id(2)

    @pl.when(c == 0)
    def _():
        state_ref[...] = jnp.zeros_like(state_ref)

    hb = opsf_ref.shape[2]
    zeros_s = jnp.zeros((HEAD_DIM, HEAD_DIM), BF16)
    zeros_v = jnp.zeros((CHUNK, HEAD_DIM), BF16)
    work = []
    for d, (ops_ref, dw_ref, chunk) in enumerate(((opsf_ref, dwf_ref, c), (opsb_ref, dwb_ref, pl.num_programs(2) - 1 - c))):
        for h in range(hb):
            ak = jnp.concatenate([ops_ref[0, 0, h, 0, OPS_AK:OPS_AK + CHUNK, :HEAD_DIM],
                                  ops_ref[0, 0, h, 0, OPS_AK:, HEAD_DIM:]], axis=0)
            work.append((d, h, ops_ref[0, 0, h, 0, :OPS_U, :], ops_ref[0, 0, h, 0, OPS_U:OPS_AK, :], ak,
                         dw_ref[0, 0, h, pl.ds(chunk, 1), :], state_ref[d * hb + h]))
    ws_all = []
    for d, h, wq, u, ak, dw, state in work:
        sb = state.astype(BF16)
        s_bd = jnp.concatenate([jnp.concatenate([sb[:, :HEAD_DIM], zeros_s], axis=1),
                                jnp.concatenate([zeros_s, sb[:, HEAD_DIM:]], axis=1)], axis=0)
        ws_all.append(jnp.dot(wq, s_bd, preferred_element_type=F32))
    akv_all = []
    for (d, h, wq, u, ak, dw, state), ws in zip(work, ws_all):
        v_new = (u.astype(F32) - ws[:CHUNK]).astype(BF16)
        v_bd = jnp.concatenate([jnp.concatenate([v_new[:, :HEAD_DIM], zeros_v], axis=1),
                                jnp.concatenate([zeros_v, v_new[:, HEAD_DIM:]], axis=1)], axis=0)
        akv_all.append(jnp.dot(ak, v_bd, preferred_element_type=F32))
    for (d, h, wq, u, ak, dw, state), ws, akv in zip(work, ws_all, akv_all):
        o_ref = ob_ref if d else of_ref
        o_ref[0, :, h * PAIR:(h + 1) * PAIR] = (ws[CHUNK:] + akv[:CHUNK]).astype(o_ref.dtype)
        state_ref[d * hb + h] = state * dw + akv[CHUNK:]


def _delta_state(ops, dw, *, heads_per_step):
    _, bsz, nkh, nc = ops.shape[:4]
    hb = heads_per_step
    s = nc * CHUNK
    return pl.pallas_call(
        _delta_state_kernel,
        out_shape=[jax.ShapeDtypeStruct((bsz, s, nkh * PAIR), BF16)] * 2,
        grid=(bsz, nkh // hb, nc),
        in_specs=[pl.BlockSpec((1, 1, hb, 1, OPS_ROWS, PAIR), lambda b, h, c: (0, b, h, c, 0, 0)),
                  pl.BlockSpec((1, 1, hb, nc, PAIR), lambda b, h, c: (0, b, h, 0, 0)),
                  pl.BlockSpec((1, 1, hb, 1, OPS_ROWS, PAIR), lambda b, h, c: (1, b, h, nc - 1 - c, 0, 0)),
                  pl.BlockSpec((1, 1, hb, nc, PAIR), lambda b, h, c: (1, b, h, 0, 0))],
        out_specs=[pl.BlockSpec((1, CHUNK, hb * PAIR), lambda b, h, c: (b, c, h)),
                   pl.BlockSpec((1, CHUNK, hb * PAIR), lambda b, h, c: (b, nc - 1 - c, h))],
        scratch_shapes=[pltpu.VMEM((2 * hb, HEAD_DIM, PAIR), F32)],
        compiler_params=_cparams(("parallel", "parallel", "arbitrary")),
        name="delta_state",
    )(ops, dw, ops, dw)


def _delta_rule(q, k, v, gates):
    bsz, s, key_dim = q.shape
    nkh = key_dim // HEAD_DIM
    nc = s // CHUNK
    g = gates[:, :, HEAD_DIM // 2:].reshape(bsz, nc, CHUNK, 2, nkh, V_PER_K)
    grow = g.transpose(0, 4, 1, 3, 5, 2).reshape(bsz, nkh, nc, 1, WIDE)
    ops, dw = _delta_intra(q, k, v, gates, grow, chunks_per_step=8)
    return _delta_state(ops, dw, heads_per_step=8)


def _dn_out_kernel(of_ref, ob_ref, z_ref, nw_ref, w_ref, h_ref, o_ref, y_ref):
    for hd in range(of_ref.shape[1] // HEAD_DIM):
        cols = slice(hd * HEAD_DIM, (hd + 1) * HEAD_DIM)
        o = of_ref[:, cols].astype(F32) + ob_ref[:, cols].astype(F32)
        z = z_ref[:, cols]
        y = _rms(o, nw_ref[...]) * (z * _sigmoid(z))
        y_ref[:, cols] = y.astype(BF16)
    o_ref[...] = h_ref[...] + jnp.dot(y_ref[...], w_ref[...], preferred_element_type=F32)


def _dn_out(o_f, o_b, z, norm_w, w_out, h, *, tm):
    m, vd = o_f.shape
    d = h.shape[1]
    return pl.pallas_call(
        _dn_out_kernel,
        out_shape=jax.ShapeDtypeStruct((m, d), F32),
        grid=(m // tm,),
        in_specs=[
            pl.BlockSpec((tm, vd), lambda i: (i, 0)),
            pl.BlockSpec((tm, vd), lambda i: (i, 0)),
            pl.BlockSpec((tm, vd), lambda i: (i, 0)),
            pl.BlockSpec((1, HEAD_DIM), lambda i: (0, 0)),
            pl.BlockSpec((vd, d), lambda i: (0, 0), pipeline_mode=pl.Buffered(1)),
            pl.BlockSpec((tm, d), lambda i: (i, 0)),
        ],
        out_specs=pl.BlockSpec((tm, d), lambda i: (i, 0)),
        scratch_shapes=[pltpu.VMEM((tm, vd), BF16)],
        compiler_params=_cparams(("parallel",)),
        name="dn_out",
    )(o_f, o_b, z, norm_w, w_out, h)


def _row(v):
    return v.reshape(1, -1).astype(F32)


def _conformer_layer(h, nw, w_pw1, b_pw1, w_dw, b_dw, ln_g, ln_b, w_pw2, b_pw2):
    bsz, s, d = h.shape
    u = _norm_glu(h.reshape(bsz * s, d), _row(nw), w_pw1.astype(BF16), _row(b_pw1), tm=1024, tn=512)
    return _conv_module(u.reshape(bsz, s, d), h, w_dw, _row(b_dw), _row(ln_g), _row(ln_b),
                        w_pw2.astype(BF16), _row(b_pw2), tm=256)


def _deltanet_layer(h, nw, w_in, w_conv, a_log, dt_bias, norm_w, w_out):
    bsz, s, d = h.shape
    key_dim = NUM_K_HEADS * HEAD_DIM
    value_dim = NUM_V_HEADS * HEAD_DIM
    zeros = jnp.zeros((2 * NUM_V_HEADS,), F32)
    alog_pad = jnp.concatenate([zeros, a_log.reshape(-1).astype(F32)]).reshape(1, -1)
    dtb_pad = jnp.concatenate([zeros, dt_bias.reshape(-1).astype(F32)]).reshape(1, -1)
    q, k, v, z, gates = _dn_in(h, _row(nw), w_in.astype(BF16), w_conv, alog_pad, dtb_pad, tm=512, tn=1024,
                               key_dim=key_dim, value_dim=value_dim)
    o_f, o_b = _delta_rule(q, k, v, gates)
    out = _dn_out(o_f.reshape(bsz * s, value_dim), o_b.reshape(bsz * s, value_dim),
                  z.reshape(bsz * s, value_dim), _row(norm_w), w_out.astype(BF16), h.reshape(bsz * s, d), tm=256)
    return out.reshape(bsz, s, d)


def kernel(x, mix_norm, ffn_norm, final_norm, cv_w_pw1, cv_b_pw1, cv_w_dw, cv_b_dw, cv_ln_g, cv_ln_b,
           cv_w_pw2, cv_b_pw2, dn_w_in, dn_w_conv, dn_a_log, dn_dt_bias, dn_norm_w, dn_w_out,
           ffn_w_gate_up, ffn_w_down):
    bsz, s, d = x.shape
    depth = mix_norm.shape[0]
    h = x
    for i in range(depth):
        j = i // 2
        if i % 2 == 0:
            h = _conformer_layer(h, mix_norm[i], cv_w_pw1[j], cv_b_pw1[j], cv_w_dw[j], cv_b_dw[j],
                                 cv_ln_g[j], cv_ln_b[j], cv_w_pw2[j], cv_b_pw2[j])
        else:
            h = _deltanet_layer(h, mix_norm[i], dn_w_in[j], dn_w_conv[j], dn_a_log[j], dn_dt_bias[j],
                                dn_norm_w[j], dn_w_out[j])
        last = i == depth - 1
        h = _ffn(h.reshape(bsz * s, d), _row(ffn_norm[i]), ffn_w_gate_up[i].astype(BF16),
                 ffn_w_down[i].astype(BF16), _row(final_norm), tm=512, th=512,
                 final_norm=last).reshape(bsz, s, d)
    return h
```

```python
import functools

import jax
import jax.numpy as jnp
from jax import lax
from jax.experimental import pallas as pl
from jax.experimental.pallas import tpu as pltpu

F32 = jnp.float32
BF16 = jnp.bfloat16

RMS_EPS = 1e-6
LN_EPS = 1e-5
L2_EPS = 1e-6

CONV_KERNEL = 31
CONV_HALO = 16
SHORT_CONV = 5
SHORT_HALO = 8

SUBLANES = 8
HEAD_DIM = 128
NUM_K_HEADS = 16
NUM_V_HEADS = 32
V_PER_K = NUM_V_HEADS // NUM_K_HEADS
CHUNK = 64

VMEM_LIMIT = 56 * 1024 * 1024


def _cparams(sem):
    return pltpu.CompilerParams(dimension_semantics=sem, vmem_limit_bytes=VMEM_LIMIT)


def _mm(a, b):
    return jnp.dot(a.astype(BF16), b.astype(BF16), preferred_element_type=F32)


def _mm_nt(a, b):
    return lax.dot_general(a.astype(BF16), b.astype(BF16), (((1,), (1,)), ((), ())),
                           preferred_element_type=F32)


def _rms(x, w):
    ms = jnp.mean(x * x, axis=-1, keepdims=True)
    return x * lax.rsqrt(ms + RMS_EPS) * w


def _sigmoid(x):
    return 1.0 / (1.0 + jnp.exp(-x))


def _in_proj_kernel(h_ref, nw_ref, w_ref, wt_ref, o_ref, tail_ref, hn_ref):
    @pl.when(pl.program_id(1) == 0)
    def _():
        hn = _rms(h_ref[...], nw_ref[...]).astype(BF16)
        hn_ref[...] = hn
        tail_ref[...] = jnp.dot(hn, wt_ref[...], preferred_element_type=F32)

    o_ref[...] = jnp.dot(hn_ref[...], w_ref[...], preferred_element_type=F32)


def _in_proj(h, nw, w, *, main, tm, tn):
    m, d = h.shape
    tail = w.shape[1] - main
    return pl.pallas_call(
        _in_proj_kernel,
        out_shape=[jax.ShapeDtypeStruct((m, main), F32), jax.ShapeDtypeStruct((m, tail), F32)],
        grid=(m // tm, main // tn),
        in_specs=[
            pl.BlockSpec((tm, d), lambda i, j: (i, 0)),
            pl.BlockSpec((1, d), lambda i, j: (0, 0)),
            pl.BlockSpec((d, tn), lambda i, j: (0, j)),
            pl.BlockSpec((d, tail), lambda i, j: (0, main // tail)),
        ],
        out_specs=[pl.BlockSpec((tm, tn), lambda i, j: (i, j)),
                   pl.BlockSpec((tm, tail), lambda i, j: (i, 0))],
        scratch_shapes=[pltpu.VMEM((tm, d), BF16)],
        compiler_params=_cparams(("parallel", "arbitrary")),
        name="in_proj",
    )(h, nw, w, w)


def _norm_glu_kernel(h_ref, nw_ref, wa_ref, wg_ref, ba_ref, bg_ref, o_ref, hn_ref):
    @pl.when(pl.program_id(1) == 0)
    def _():
        hn_ref[...] = _rms(h_ref[...], nw_ref[...]).astype(BF16)

    hn = hn_ref[...]
    a = jnp.dot(hn, wa_ref[...], preferred_element_type=F32) + ba_ref[...]
    g = jnp.dot(hn, wg_ref[...], preferred_element_type=F32) + bg_ref[...]
    o_ref[...] = a * _sigmoid(g)


def _norm_glu(h, nw, w, b, *, tm, tn):
    m, d = h.shape
    n = w.shape[1] // 2
    nb = n // tn
    return pl.pallas_call(
        _norm_glu_kernel,
        out_shape=jax.ShapeDtypeStruct((m, n), F32),
        grid=(m // tm, nb),
        in_specs=[
            pl.BlockSpec((tm, d), lambda i, j: (i, 0)),
            pl.BlockSpec((1, d), lambda i, j: (0, 0)),
            pl.BlockSpec((d, tn), lambda i, j: (0, j)),
            pl.BlockSpec((d, tn), lambda i, j: (0, j + nb)),
            pl.BlockSpec((1, tn), lambda i, j: (0, j)),
            pl.BlockSpec((1, tn), lambda i, j: (0, j + nb)),
        ],
        out_specs=pl.BlockSpec((tm, tn), lambda i, j: (i, j)),
        scratch_shapes=[pltpu.VMEM((tm, d), BF16)],
        compiler_params=_cparams(("parallel", "arbitrary")),
        name="norm_glu",
    )(h, nw, w, w, b, b)


_CONV_ROWS = 64
_CONV_LANES = 256


def _conv_module_kernel(uc_ref, up_ref, un_ref, h_ref, wdw_ref, bdw_ref, g_ref, b_ref, w2_ref, b2_ref,
                        o_ref, ext_ref, cv_ref, sh_ref):
    i = pl.program_id(1)
    tm = uc_ref.shape[1]
    d = uc_ref.shape[2]
    pad = (CONV_KERNEL - 1) // 2
    ext_ref[0:CONV_HALO, :] = jnp.where(i > 0, up_ref[0], 0.0)
    ext_ref[CONV_HALO:CONV_HALO + tm, :] = uc_ref[0]
    ext_ref[CONV_HALO + tm:, :] = jnp.where(i < pl.num_programs(1) - 1, un_ref[0], 0.0)

    base = CONV_HALO - pad
    span = sh_ref.shape[1]

    def lane_block(cb, carry):
        cols = pl.ds(pl.multiple_of(cb * _CONV_LANES, _CONV_LANES), _CONV_LANES)
        for s in range(1, SUBLANES):
            sh_ref[s - 1] = ext_ref[s:s + span, cols]
        for r0 in range(0, tm, _CONV_ROWS):
            acc = jnp.zeros((_CONV_ROWS, _CONV_LANES), F32)
            for j in range(CONV_KERNEL):
                s = (base + j) % SUBLANES
                a = r0 + base + j - s
                rows = ext_ref[a:a + _CONV_ROWS, cols] if s == 0 else sh_ref[s - 1, a:a + _CONV_ROWS, :]
                acc = acc + wdw_ref[j:j + 1, cols] * rows
            cv_ref[r0:r0 + _CONV_ROWS, cols] = acc + bdw_ref[:, cols]
        return carry

    lax.fori_loop(0, d // _CONV_LANES, lane_block, 0)

    x = cv_ref[...]
    mu = jnp.mean(x, axis=-1, keepdims=True)
    xc = x - mu
    var = jnp.mean(xc * xc, axis=-1, keepdims=True)
    y = xc * lax.rsqrt(var + LN_EPS) * g_ref[...] + b_ref[...]
    y = y * _sigmoid(y)
    o_ref[0] = h_ref[0] + jnp.dot(y.astype(BF16), w2_ref[...], preferred_element_type=F32) + b2_ref[...]


def _conv_module(u, h, wdw, bdw, ln_g, ln_b, w2, b2, *, tm):
    bsz, s, d = u.shape
    nh = tm // CONV_HALO
    last = s // CONV_HALO - 1
    tile = pl.BlockSpec((1, tm, d), lambda b, i: (b, i, 0))
    vec = pl.BlockSpec((1, d), lambda b, i: (0, 0))
    return pl.pallas_call(
        _conv_module_kernel,
        out_shape=jax.ShapeDtypeStruct((bsz, s, d), F32),
        grid=(bsz, s // tm),
        in_specs=[
            tile,
            pl.BlockSpec((1, CONV_HALO, d), lambda b, i: (b, jnp.maximum(i * nh - 1, 0), 0)),
            pl.BlockSpec((1, CONV_HALO, d), lambda b, i: (b, jnp.minimum((i + 1) * nh, last), 0)),
            tile,
            pl.BlockSpec((CONV_KERNEL, d), lambda b, i: (0, 0)),
            vec, vec, vec,
            pl.BlockSpec((d, d), lambda b, i: (0, 0), pipeline_mode=pl.Buffered(1)),
            vec,
        ],
        out_specs=tile,
        scratch_shapes=[pltpu.VMEM((tm + 2 * CONV_HALO, d), F32), pltpu.VMEM((tm, d), F32),
                        pltpu.VMEM((SUBLANES - 1, tm + 2 * CONV_HALO - SUBLANES, _CONV_LANES), F32)],
        compiler_params=_cparams(("parallel", "parallel")),
        name="conv_module",
    )(u, u, u, h, wdw, bdw, ln_g, ln_b, w2, b2)


def _ffn_kernel(h_ref, nw_ref, wg_ref, wu_ref, wd_ref, fw_ref, o_ref, hn_ref, acc_ref, *, final_norm):
    j = pl.program_id(1)

    @pl.when(j == 0)
    def _():
        hn_ref[...] = _rms(h_ref[...], nw_ref[...]).astype(BF16)
        acc_ref[...] = jnp.zeros_like(acc_ref)

    hn = hn_ref[...]
    g = jnp.dot(hn, wg_ref[...], preferred_element_type=F32)
    u = jnp.dot(hn, wu_ref[...], preferred_element_type=F32)
    a = (g * _sigmoid(g) * u).astype(BF16)
    acc_ref[...] += jnp.dot(a, wd_ref[...], preferred_element_type=F32)

    @pl.when(j == pl.num_programs(1) - 1)
    def _():
        y = h_ref[...] + acc_ref[...]
        if final_norm:
            y = _rms(y, fw_ref[...])
        o_ref[...] = y


def _ffn(h, nw, w_gate_up, w_down, fw, *, tm, th, final_norm):
    m, d = h.shape
    hid = w_down.shape[0]
    nb = hid // th
    vec = pl.BlockSpec((1, d), lambda i, j: (0, 0))
    return pl.pallas_call(
        functools.partial(_ffn_kernel, final_norm=final_norm),
        out_shape=jax.ShapeDtypeStruct((m, d), F32),
        grid=(m // tm, nb),
        in_specs=[
            pl.BlockSpec((tm, d), lambda i, j: (i, 0)),
            vec,
            pl.BlockSpec((d, th), lambda i, j: (0, j)),
            pl.BlockSpec((d, th), lambda i, j: (0, j + nb)),
            pl.BlockSpec((th, d), lambda i, j: (j, 0)),
            vec,
        ],
        out_specs=pl.BlockSpec((tm, d), lambda i, j: (i, 0)),
        scratch_shapes=[pltpu.VMEM((tm, d), BF16), pltpu.VMEM((tm, d), F32)],
        compiler_params=_cparams(("parallel", "arbitrary")),
        name="ffn",
    )(h, nw, w_gate_up, w_gate_up, w_down, fw)


def _dn_prep_kernel(xc_ref, xp_ref, xn_ref, wc_ref, ba_ref, alog_ref, dtb_ref,
                    q_ref, k_ref, v_ref, gate_ref, ext_ref):
    i = pl.program_id(1)
    tm = xc_ref.shape[1]
    key_dim = q_ref.shape[2]
    pad = (SHORT_CONV - 1) // 2
    ext_ref[0:SHORT_HALO, :] = jnp.where(i > 0, xp_ref[0], 0.0)
    ext_ref[SHORT_HALO:SHORT_HALO + tm, :] = xc_ref[0]
    ext_ref[SHORT_HALO + tm:, :] = jnp.where(i < pl.num_programs(1) - 1, xn_ref[0], 0.0)

    base = SHORT_HALO - pad
    n_heads = xc_ref.shape[2] // HEAD_DIM
    for hd in range(n_heads):
        cols = slice(hd * HEAD_DIM, (hd + 1) * HEAD_DIM)
        acc = jnp.zeros((tm, HEAD_DIM), F32)
        for j in range(SHORT_CONV):
            acc = acc + wc_ref[j:j + 1, cols] * ext_ref[base + j:base + j + tm, cols]
        y = acc * _sigmoid(acc)
        c0 = hd * HEAD_DIM
        if c0 < 2 * key_dim:
            y = y * lax.rsqrt(jnp.sum(y * y, axis=-1, keepdims=True) + L2_EPS)
            if c0 < key_dim:
                q_ref[0, :, cols] = (y * (HEAD_DIM ** -0.5)).astype(BF16)
            else:
                k_ref[0, :, c0 - key_dim:c0 - key_dim + HEAD_DIM] = y.astype(BF16)
        else:
            v_ref[0, :, c0 - 2 * key_dim:c0 - 2 * key_dim + HEAD_DIM] = y.astype(BF16)

    raw = ba_ref[0]
    beta = _sigmoid(raw)
    z = raw + dtb_ref[...]
    softplus = jnp.maximum(z, 0.0) + jnp.log(1.0 + jnp.exp(-jnp.abs(z)))
    g = -jnp.exp(alog_ref[...]) * softplus
    r = lax.broadcasted_iota(jnp.int32, (2 * CHUNK, CHUNK), 0)
    c = lax.broadcasted_iota(jnp.int32, (2 * CHUNK, CHUNK), 1)
    tri = ((r < CHUNK) & (c <= r)) | ((r >= CHUNK) & (c >= r - CHUNK))
    tri = jnp.where(tri, 1.0, 0.0).astype(BF16)
    g_hi = g.astype(BF16)
    rest = g - g_hi.astype(F32)
    g_mid = rest.astype(BF16)
    g_lo = (rest - g_mid.astype(F32)).astype(BF16)
    quarter = HEAD_DIM // 4
    lane = lax.broadcasted_iota(jnp.int32, (CHUNK, HEAD_DIM), 1)
    for n in range(tm // CHUNK):
        rows = slice(n * CHUNK, (n + 1) * CHUNK)
        sums = (jnp.dot(tri, g_hi[rows], preferred_element_type=F32)
                + jnp.dot(tri, g_mid[rows], preferred_element_type=F32)
                + jnp.dot(tri, g_lo[rows], preferred_element_type=F32))
        gate_ref[0, rows, :] = jnp.where(lane < 2 * quarter, beta[rows],
                                         jnp.where(lane < 3 * quarter, sums[:CHUNK], sums[CHUNK:]))


def _dn_prep(proj, ba, w_conv, alog_pad, dtb_pad, *, tm, key_dim, value_dim):
    bsz, s, _ = proj.shape
    qkv_dim = 2 * key_dim + value_dim
    nh = tm // SHORT_HALO
    last = s // SHORT_HALO - 1
    return pl.pallas_call(
        _dn_prep_kernel,
        out_shape=[
            jax.ShapeDtypeStruct((bsz, s, key_dim), BF16),
            jax.ShapeDtypeStruct((bsz, s, key_dim), BF16),
            jax.ShapeDtypeStruct((bsz, s, value_dim), BF16),
            jax.ShapeDtypeStruct((bsz, s, HEAD_DIM), F32),
        ],
        grid=(bsz, s // tm),
        in_specs=[
            pl.BlockSpec((1, tm, qkv_dim), lambda b, i: (b, i, 0)),
            pl.BlockSpec((1, SHORT_HALO, qkv_dim), lambda b, i: (b, jnp.maximum(i * nh - 1, 0), 0)),
            pl.BlockSpec((1, SHORT_HALO, qkv_dim), lambda b, i: (b, jnp.minimum((i + 1) * nh, last), 0)),
            pl.BlockSpec((SHORT_CONV, qkv_dim), lambda b, i: (0, 0)),
            pl.BlockSpec((1, tm, HEAD_DIM), lambda b, i: (b, i, 0)),
            pl.BlockSpec((1, HEAD_DIM), lambda b, i: (0, 0)),
            pl.BlockSpec((1, HEAD_DIM), lambda b, i: (0, 0)),
        ],
        out_specs=[
            pl.BlockSpec((1, tm, key_dim), lambda b, i: (b, i, 0)),
            pl.BlockSpec((1, tm, key_dim), lambda b, i: (b, i, 0)),
            pl.BlockSpec((1, tm, value_dim), lambda b, i: (b, i, 0)),
            pl.BlockSpec((1, tm, HEAD_DIM), lambda b, i: (b, i, 0)),
        ],
        scratch_shapes=[pltpu.VMEM((tm + 2 * SHORT_HALO, qkv_dim), F32)],
        compiler_params=_cparams(("parallel", "parallel")),
        name="dn_prep",
    )(proj, proj, proj, w_conv, ba, alog_pad, dtb_pad)


N_CHAINS = 2 * V_PER_K
WIDE = N_CHAINS * CHUNK
PAIR = V_PER_K * HEAD_DIM
OPS_U = 2 * CHUNK
OPS_AK = 3 * CHUNK
OPS_ROWS = OPS_AK + HEAD_DIM
_INTRA_GROUP = 8


def _block_diag(x):
    n = x.shape[1]
    rb = lax.broadcasted_iota(jnp.int32, (n, n), 0) // CHUNK
    cb = lax.broadcasted_iota(jnp.int32, (n, n), 1) // CHUNK
    tiled = jnp.concatenate([x.astype(BF16)] * N_CHAINS, axis=0)
    return jnp.where(rb == cb, tiled, jnp.zeros_like(tiled))


def _delta_intra_kernel(q_ref, k_ref, v_ref, g_ref, grow_ref, ops_ref, dw_ref):
    kh = pl.program_id(1)
    cs = CHUNK
    quarter = HEAD_DIM // 4
    chunks = range(q_ref.shape[1] // cs)

    r = lax.broadcasted_iota(jnp.int32, (cs, WIDE), 0)
    lw = lax.broadcasted_iota(jnp.int32, (cs, WIDE), 1)
    blk = lw // cs
    c = lw - blk * cs
    ahead = jnp.where(blk < V_PER_K, r - c, c - r)
    incl = ahead >= 0
    strict = ahead > 0
    eye = jnp.where(r == c, 1.0, 0.0)
    lane_g = lax.broadcasted_iota(jnp.int32, (cs, HEAD_DIM), 1)

    def wide(cols):
        out = cols[N_CHAINS - 1]
        for j in range(N_CHAINS - 2, -1, -1):
            out = jnp.where(blk == j, cols[j], out)
        return out

    q, k, v, grow, beta_c, g_c, decay, p, t, a_w = ([None] * len(chunks) for _ in range(10))

    def prepare(n):
        rows = slice(n * cs, (n + 1) * cs)
        q[n], k[n], v[n], grow[n] = q_ref[0, rows, :], k_ref[0, rows, :], v_ref[0, rows, :], grow_ref[0, 0, n]
        gates = g_ref[0, rows, :]
        beta_c[n], g_c[n] = [], []
        for j in range(N_CHAINS):
            d, i = divmod(j, V_PER_K)
            head = kh * V_PER_K + i
            beta_c[n].append(jnp.sum(jnp.where(lane_g == d * quarter + head, gates, 0.0), axis=1, keepdims=True))
            g_c[n].append(jnp.sum(jnp.where(lane_g == (2 + d) * quarter + head, gates, 0.0), axis=1,
                                  keepdims=True))
        decay[n] = jnp.where(incl, jnp.exp(jnp.where(incl, wide(g_c[n]) - grow[n], 0.0)), 0.0)
        qkk = _mm_nt(jnp.concatenate([q[n], k[n]], axis=0), jnp.concatenate([k[n]] * N_CHAINS, axis=0))
        p[n] = -jnp.where(strict, qkk[cs:] * decay[n] * wide(beta_c[n]), 0.0)
        t[n] = eye + p[n]
        a_w[n] = jnp.where(incl, qkk[:cs] * decay[n], 0.0)

    def first_level(n):
        p[n] = _mm(p[n], _block_diag(p[n]))

    def middle_level(n):
        both = _mm(jnp.concatenate([p[n], t[n]], axis=0), _block_diag(p[n]))
        p[n] = both[:cs]
        t[n] = t[n] + both[cs:]

    def last_level(n):
        t[n] = t[n] + _mm(t[n], _block_diag(p[n]))

    levels = [first_level] + [middle_level] * (cs.bit_length() - 3) + [last_level]

    def finish(n):
        kf = k[n].astype(F32)
        qf = q[n].astype(F32)
        rhs, qg, kg, dwl = [], [], [], []
        for j in range(N_CHAINS):
            d, i = divmod(j, V_PER_K)
            eg = jnp.exp(g_c[n][j])
            vj = v[n][:, i * HEAD_DIM:(i + 1) * HEAD_DIM].astype(F32)
            rhs.append(jnp.concatenate([vj * beta_c[n][j], kf * (beta_c[n][j] * eg)], axis=1).astype(BF16))
            g_last = grow[n][:, j * cs:j * cs + 1] if d == 1 else grow[n][:, (j + 1) * cs - 1:(j + 1) * cs]
            qg.append(qf * eg)
            kg.append(kf * jnp.exp(g_last - g_c[n][j]))
            dwl.append(jnp.broadcast_to(jnp.exp(g_last), (1, HEAD_DIM)))
        lhs = jnp.concatenate([jnp.where(blk == j, t[n], 0.0) for j in range(N_CHAINS)], axis=0)
        uw = _mm(lhs, jnp.concatenate(rhs, axis=0))
        for d in range(2):
            j0, j1 = d * V_PER_K, d * V_PER_K + 1
            u0, w0 = uw[j0 * cs:(j0 + 1) * cs, :HEAD_DIM], uw[j0 * cs:(j0 + 1) * cs, HEAD_DIM:]
            u1, w1 = uw[j1 * cs:(j1 + 1) * cs, :HEAD_DIM], uw[j1 * cs:(j1 + 1) * cs, HEAD_DIM:]
            kgt = jnp.concatenate([kg[j0], kg[j1]], axis=0).T
            a_pad = jnp.concatenate([a_w[n][:, d * 2 * cs:(d + 1) * 2 * cs], jnp.zeros((HEAD_DIM - cs, 2 * cs), F32)],
                                    axis=0)
            ops = jnp.concatenate([jnp.concatenate([w0, w1], axis=1),
                                   jnp.concatenate([qg[j0], qg[j1]], axis=1),
                                   jnp.concatenate([u0, u1], axis=1),
                                   jnp.concatenate([a_pad, kgt], axis=1)], axis=0)
            ops_ref[d, 0, 0, n] = ops.astype(BF16)
            dw_ref[d, 0, 0, n:n + 1, :] = jnp.concatenate([dwl[j0], dwl[j1]], axis=1)

    groups = [list(chunks)[g:g + _INTRA_GROUP] for g in range(0, len(chunks), _INTRA_GROUP)]
    for stage in range(len(groups) + 2):
        inverse = groups[stage - 1] if 1 <= stage <= len(groups) else []
        filler = [functools.partial(finish, n) for n in (groups[stage - 2] if stage >= 2 else [])]
        filler += [functools.partial(prepare, n) for n in (groups[stage] if stage < len(groups) else [])]
        per_level = -(-len(filler) // len(levels))
        for li, level in enumerate(levels):
            for n in inverse:
                level(n)
            for task in filler[li * per_level:(li + 1) * per_level]:
                task()


def _delta_intra(q, k, v, gates, grow, *, chunks_per_step):
    bsz, s, key_dim = q.shape
    nkh = key_dim // HEAD_DIM
    nc = s // CHUNK
    tc = chunks_per_step
    tm = tc * CHUNK
    tok = lambda w: pl.BlockSpec((1, tm, w), lambda b, h, n: (b, n, h))
    return pl.pallas_call(
        _delta_intra_kernel,
        out_shape=[jax.ShapeDtypeStruct((2, bsz, nkh, nc, OPS_ROWS, PAIR), BF16),
                   jax.ShapeDtypeStruct((2, bsz, nkh, nc, PAIR), F32)],
        grid=(bsz, nkh, nc // tc),
        in_specs=[tok(HEAD_DIM), tok(HEAD_DIM), tok(PAIR),
                  pl.BlockSpec((1, tm, HEAD_DIM), lambda b, h, n: (b, n, 0)),
                  pl.BlockSpec((1, 1, tc, 1, WIDE), lambda b, h, n: (b, h, n, 0, 0))],
        out_specs=[pl.BlockSpec((2, 1, 1, tc, OPS_ROWS, PAIR), lambda b, h, n: (0, b, h, n, 0, 0)),
                   pl.BlockSpec((2, 1, 1, tc, PAIR), lambda b, h, n: (0, b, h, n, 0))],
        compiler_params=_cparams(("parallel", "parallel", "parallel")),
        name="delta_intra",
    )(q, k, v, gates, grow)


def _delta_state_kernel(opsf_ref, dwf_ref, opsb_ref, dwb_ref, of_ref, ob_ref, state_ref):
    c = pl.program_id(2)

    @pl.when(c == 0)
    def _():
        state_ref[...] = jnp.zeros_like(state_ref)

    hb = opsf_ref.shape[2]
    zeros_s = jnp.zeros((HEAD_DIM, HEAD_DIM), BF16)
    zeros_v = jnp.zeros((CHUNK, HEAD_DIM), BF16)
    work = []
    for d, (ops_ref, dw_ref, chunk) in enumerate(((opsf_ref, dwf_ref, c), (opsb_ref, dwb_ref, pl.num_programs(2) - 1 - c))):
        for h in range(hb):
            ak = jnp.concatenate([ops_ref[0, 0, h, 0, OPS_AK:OPS_AK + CHUNK, :HEAD_DIM],
                                  ops_ref[0, 0, h, 0, OPS_AK:, HEAD_DIM:]], axis=0)
            work.append((d, h, ops_ref[0, 0, h, 0, :OPS_U, :], ops_ref[0, 0, h, 0, OPS_U:OPS_AK, :], ak,
                         dw_ref[0, 0, h, pl.ds(chunk, 1), :], state_ref[d * hb + h]))
    ws_all = []
    for d, h, wq, u, ak, dw, state in work:
        sb = state.astype(BF16)
        s_bd = jnp.concatenate([jnp.concatenate([sb[:, :HEAD_DIM], zeros_s], axis=1),
                                jnp.concatenate([zeros_s, sb[:, HEAD_DIM:]], axis=1)], axis=0)
        ws_all.append(jnp.dot(wq, s_bd, preferred_element_type=F32))
    akv_all = []
    for (d, h, wq, u, ak, dw, state), ws in zip(work, ws_all):
        v_new = (u.astype(F32) - ws[:CHUNK]).astype(BF16)
        v_bd = jnp.concatenate([jnp.concatenate([v_new[:, :HEAD_DIM], zeros_v], axis=1),
                                jnp.concatenate([zeros_v, v_new[:, HEAD_DIM:]], axis=1)], axis=0)
        akv_all.append(jnp.dot(ak, v_bd, preferred_element_type=F32))
    for (d, h, wq, u, ak, dw, state), ws, akv in zip(work, ws_all, akv_all):
        o_ref = ob_ref if d else of_ref
        o_ref[0, :, h * PAIR:(h + 1) * PAIR] = (ws[CHUNK:] + akv[:CHUNK]).astype(o_ref.dtype)
        state_ref[d * hb + h] = state * dw + akv[CHUNK:]


def _delta_state(ops, dw, *, heads_per_step):
    _, bsz, nkh, nc = ops.shape[:4]
    hb = heads_per_step
    s = nc * CHUNK
    return pl.pallas_call(
        _delta_state_kernel,
        out_shape=[jax.ShapeDtypeStruct((bsz, s, nkh * PAIR), BF16)] * 2,
        grid=(bsz, nkh // hb, nc),
        in_specs=[pl.BlockSpec((1, 1, hb, 1, OPS_ROWS, PAIR), lambda b, h, c: (0, b, h, c, 0, 0)),
                  pl.BlockSpec((1, 1, hb, nc, PAIR), lambda b, h, c: (0, b, h, 0, 0)),
                  pl.BlockSpec((1, 1, hb, 1, OPS_ROWS, PAIR), lambda b, h, c: (1, b, h, nc - 1 - c, 0, 0)),
                  pl.BlockSpec((1, 1, hb, nc, PAIR), lambda b, h, c: (1, b, h, 0, 0))],
        out_specs=[pl.BlockSpec((1, CHUNK, hb * PAIR), lambda b, h, c: (b, c, h)),
                   pl.BlockSpec((1, CHUNK, hb * PAIR), lambda b, h, c: (b, nc - 1 - c, h))],
        scratch_shapes=[pltpu.VMEM((2 * hb, HEAD_DIM, PAIR), F32)],
        compiler_params=_cparams(("parallel", "parallel", "arbitrary")),
        name="delta_state",
    )(ops, dw, ops, dw)


def _delta_rule(q, k, v, gates):
    bsz, s, key_dim = q.shape
    nkh = key_dim // HEAD_DIM
    nc = s // CHUNK
    g = gates[:, :, HEAD_DIM // 2:].reshape(bsz, nc, CHUNK, 2, nkh, V_PER_K)
    grow = g.transpose(0, 4, 1, 3, 5, 2).reshape(bsz, nkh, nc, 1, WIDE)
    ops, dw = _delta_intra(q, k, v, gates, grow, chunks_per_step=16)
    return _delta_state(ops, dw, heads_per_step=16)


def _dn_out_kernel(of_ref, ob_ref, z_ref, nw_ref, w_ref, h_ref, o_ref, y_ref):
    for hd in range(of_ref.shape[1] // HEAD_DIM):
        cols = slice(hd * HEAD_DIM, (hd + 1) * HEAD_DIM)
        o = of_ref[:, cols].astype(F32) + ob_ref[:, cols].astype(F32)
        z = z_ref[:, cols]
        y = _rms(o, nw_ref[...]) * (z * _sigmoid(z))
        y_ref[:, cols] = y.astype(BF16)
    o_ref[...] = h_ref[...] + jnp.dot(y_ref[...], w_ref[...], preferred_element_type=F32)


def _dn_out(o_f, o_b, proj, z_block, norm_w, w_out, h, *, tm):
    m, vd = o_f.shape
    d = h.shape[1]
    return pl.pallas_call(
        _dn_out_kernel,
        out_shape=jax.ShapeDtypeStruct((m, d), F32),
        grid=(m // tm,),
        in_specs=[
            pl.BlockSpec((tm, vd), lambda i: (i, 0)),
            pl.BlockSpec((tm, vd), lambda i: (i, 0)),
            pl.BlockSpec((tm, vd), lambda i: (i, z_block)),
            pl.BlockSpec((1, HEAD_DIM), lambda i: (0, 0)),
            pl.BlockSpec((vd, d), lambda i: (0, 0), pipeline_mode=pl.Buffered(1)),
            pl.BlockSpec((tm, d), lambda i: (i, 0)),
        ],
        out_specs=pl.BlockSpec((tm, d), lambda i: (i, 0)),
        scratch_shapes=[pltpu.VMEM((tm, vd), BF16)],
        compiler_params=_cparams(("parallel",)),
        name="dn_out",
    )(o_f, o_b, proj, norm_w, w_out, h)


def _row(v):
    return v.reshape(1, -1).astype(F32)


def _conformer_layer(h, nw, w_pw1, b_pw1, w_dw, b_dw, ln_g, ln_b, w_pw2, b_pw2):
    bsz, s, d = h.shape
    u = _norm_glu(h.reshape(bsz * s, d), _row(nw), w_pw1.astype(BF16), _row(b_pw1), tm=1024, tn=512)
    return _conv_module(u.reshape(bsz, s, d), h, w_dw, _row(b_dw), _row(ln_g), _row(ln_b),
                        w_pw2.astype(BF16), _row(b_pw2), tm=256)


def _deltanet_layer(h, nw, w_in, w_conv, a_log, dt_bias, norm_w, w_out):
    bsz, s, d = h.shape
    key_dim = NUM_K_HEADS * HEAD_DIM
    value_dim = NUM_V_HEADS * HEAD_DIM
    main = 2 * key_dim + value_dim + value_dim
    h2 = h.reshape(bsz * s, d)
    proj, ba = _in_proj(h2, _row(nw), w_in.astype(BF16), main=main, tm=1024, tn=1024)
    zeros = jnp.zeros((2 * NUM_V_HEADS,), F32)
    alog_pad = jnp.concatenate([zeros, a_log.reshape(-1).astype(F32)]).reshape(1, -1)
    dtb_pad = jnp.concatenate([zeros, dt_bias.reshape(-1).astype(F32)]).reshape(1, -1)
    q, k, v, gates = _dn_prep(proj.reshape(bsz, s, main), ba.reshape(bsz, s, -1), w_conv, alog_pad, dtb_pad,
                              tm=256, key_dim=key_dim, value_dim=value_dim)
    o_f, o_b = _delta_rule(q, k, v, gates)
    out = _dn_out(o_f.reshape(bsz * s, value_dim), o_b.reshape(bsz * s, value_dim), proj,
                  (2 * key_dim + value_dim) // value_dim, _row(norm_w), w_out.astype(BF16), h2, tm=256)
    return out.reshape(bsz, s, d)


def kernel(x, mix_norm, ffn_norm, final_norm, cv_w_pw1, cv_b_pw1, cv_w_dw, cv_b_dw, cv_ln_g, cv_ln_b,
           cv_w_pw2, cv_b_pw2, dn_w_in, dn_w_conv, dn_a_log, dn_dt_bias, dn_norm_w, dn_w_out,
           ffn_w_gate_up, ffn_w_down):
    bsz, s, d = x.shape
    depth = mix_norm.shape[0]
    h = x
    for i in range(depth):
        j = i // 2
        if i % 2 == 0:
            h = _conformer_layer(h, mix_norm[i], cv_w_pw1[j], cv_b_pw1[j], cv_w_dw[j], cv_b_dw[j],
                                 cv_ln_g[j], cv_ln_b[j], cv_w_pw2[j], cv_b_pw2[j])
        else:
            h = _deltanet_layer(h, mix_norm[i], dn_w_in[j], dn_w_conv[j], dn_a_log[j], dn_dt_bias[j],
                                dn_norm_w[j], dn_w_out[j])
        last = i == depth - 1
        h = _ffn(h.reshape(bsz * s, d), _row(ffn_norm[i]), ffn_w_gate_up[i].astype(BF16),
                 ffn_w_down[i].astype(BF16), _row(final_norm), tm=512, th=512,
                 final_norm=last).reshape(bsz, s, d)
    return h
```

```python
import functools

import jax
import jax.numpy as jnp
from jax import lax
from jax.experimental import pallas as pl
from jax.experimental.pallas import tpu as pltpu

F32 = jnp.float32
BF16 = jnp.bfloat16

RMS_EPS = 1e-6
LN_EPS = 1e-5
L2_EPS = 1e-6

CONV_KERNEL = 31
CONV_HALO = 16
SHORT_CONV = 5
SHORT_HALO = 8

SUBLANES = 8
HEAD_DIM = 128
NUM_K_HEADS = 16
NUM_V_HEADS = 32
V_PER_K = NUM_V_HEADS // NUM_K_HEADS
CHUNK = 64

VMEM_LIMIT = 56 * 1024 * 1024


def _cparams(sem):
    return pltpu.CompilerParams(dimension_semantics=sem, vmem_limit_bytes=VMEM_LIMIT)


def _mm(a, b):
    return jnp.dot(a.astype(BF16), b.astype(BF16), preferred_element_type=F32)


def _mm_nt(a, b):
    return lax.dot_general(a.astype(BF16), b.astype(BF16), (((1,), (1,)), ((), ())),
                           preferred_element_type=F32)


def _rms(x, w):
    ms = jnp.mean(x * x, axis=-1, keepdims=True)
    return x * lax.rsqrt(ms + RMS_EPS) * w


def _sigmoid(x):
    return 1.0 / (1.0 + jnp.exp(-x))


def _in_proj_kernel(h_ref, nw_ref, w_ref, wt_ref, o_ref, tail_ref, hn_ref):
    @pl.when(pl.program_id(1) == 0)
    def _():
        hn = _rms(h_ref[...], nw_ref[...]).astype(BF16)
        hn_ref[...] = hn
        tail_ref[...] = jnp.dot(hn, wt_ref[...], preferred_element_type=F32)

    o_ref[...] = jnp.dot(hn_ref[...], w_ref[...], preferred_element_type=F32)


def _in_proj(h, nw, w, *, main, tm, tn):
    m, d = h.shape
    tail = w.shape[1] - main
    return pl.pallas_call(
        _in_proj_kernel,
        out_shape=[jax.ShapeDtypeStruct((m, main), F32), jax.ShapeDtypeStruct((m, tail), F32)],
        grid=(m // tm, main // tn),
        in_specs=[
            pl.BlockSpec((tm, d), lambda i, j: (i, 0)),
            pl.BlockSpec((1, d), lambda i, j: (0, 0)),
            pl.BlockSpec((d, tn), lambda i, j: (0, j)),
            pl.BlockSpec((d, tail), lambda i, j: (0, main // tail)),
        ],
        out_specs=[pl.BlockSpec((tm, tn), lambda i, j: (i, j)),
                   pl.BlockSpec((tm, tail), lambda i, j: (i, 0))],
        scratch_shapes=[pltpu.VMEM((tm, d), BF16)],
        compiler_params=_cparams(("parallel", "arbitrary")),
        name="in_proj",
    )(h, nw, w, w)


def _norm_glu_kernel(h_ref, nw_ref, wa_ref, wg_ref, ba_ref, bg_ref, o_ref, hn_ref):
    @pl.when(pl.program_id(1) == 0)
    def _():
        hn_ref[...] = _rms(h_ref[...], nw_ref[...]).astype(BF16)

    hn = hn_ref[...]
    a = jnp.dot(hn, wa_ref[...], preferred_element_type=F32) + ba_ref[...]
    g = jnp.dot(hn, wg_ref[...], preferred_element_type=F32) + bg_ref[...]
    o_ref[...] = a * _sigmoid(g)


def _norm_glu(h, nw, w, b, *, tm, tn):
    m, d = h.shape
    n = w.shape[1] // 2
    nb = n // tn
    return pl.pallas_call(
        _norm_glu_kernel,
        out_shape=jax.ShapeDtypeStruct((m, n), F32),
        grid=(m // tm, nb),
        in_specs=[
            pl.BlockSpec((tm, d), lambda i, j: (i, 0)),
            pl.BlockSpec((1, d), lambda i, j: (0, 0)),
            pl.BlockSpec((d, tn), lambda i, j: (0, j)),
            pl.BlockSpec((d, tn), lambda i, j: (0, j + nb)),
            pl.BlockSpec((1, tn), lambda i, j: (0, j)),
            pl.BlockSpec((1, tn), lambda i, j: (0, j + nb)),
        ],
        out_specs=pl.BlockSpec((tm, tn), lambda i, j: (i, j)),
        scratch_shapes=[pltpu.VMEM((tm, d), BF16)],
        compiler_params=_cparams(("parallel", "arbitrary")),
        name="norm_glu",
    )(h, nw, w, w, b, b)


_CONV_ROWS = 64
_CONV_LANES = 256


def _conv_module_kernel(uc_ref, up_ref, un_ref, h_ref, wdw_ref, bdw_ref, g_ref, b_ref, w2_ref, b2_ref,
                        o_ref, ext_ref, cv_ref, sh_ref):
    i = pl.program_id(1)
    tm = uc_ref.shape[1]
    d = uc_ref.shape[2]
    pad = (CONV_KERNEL - 1) // 2
    ext_ref[0:CONV_HALO, :] = jnp.where(i > 0, up_ref[0], 0.0)
    ext_ref[CONV_HALO:CONV_HALO + tm, :] = uc_ref[0]
    ext_ref[CONV_HALO + tm:, :] = jnp.where(i < pl.num_programs(1) - 1, un_ref[0], 0.0)

    base = CONV_HALO - pad
    span = sh_ref.shape[1]

    def lane_block(cb, carry):
        cols = pl.ds(pl.multiple_of(cb * _CONV_LANES, _CONV_LANES), _CONV_LANES)
        for s in range(1, SUBLANES):
            sh_ref[s - 1] = ext_ref[s:s + span, cols]
        for r0 in range(0, tm, _CONV_ROWS):
            acc = jnp.zeros((_CONV_ROWS, _CONV_LANES), F32)
            for j in range(CONV_KERNEL):
                s = (base + j) % SUBLANES
                a = r0 + base + j - s
                rows = ext_ref[a:a + _CONV_ROWS, cols] if s == 0 else sh_ref[s - 1, a:a + _CONV_ROWS, :]
                acc = acc + wdw_ref[j:j + 1, cols] * rows
            cv_ref[r0:r0 + _CONV_ROWS, cols] = acc + bdw_ref[:, cols]
        return carry

    lax.fori_loop(0, d // _CONV_LANES, lane_block, 0)

    x = cv_ref[...]
    mu = jnp.mean(x, axis=-1, keepdims=True)
    xc = x - mu
    var = jnp.mean(xc * xc, axis=-1, keepdims=True)
    y = xc * lax.rsqrt(var + LN_EPS) * g_ref[...] + b_ref[...]
    y = y * _sigmoid(y)
    o_ref[0] = h_ref[0] + jnp.dot(y.astype(BF16), w2_ref[...], preferred_element_type=F32) + b2_ref[...]


def _conv_module(u, h, wdw, bdw, ln_g, ln_b, w2, b2, *, tm):
    bsz, s, d = u.shape
    nh = tm // CONV_HALO
    last = s // CONV_HALO - 1
    tile = pl.BlockSpec((1, tm, d), lambda b, i: (b, i, 0))
    vec = pl.BlockSpec((1, d), lambda b, i: (0, 0))
    return pl.pallas_call(
        _conv_module_kernel,
        out_shape=jax.ShapeDtypeStruct((bsz, s, d), F32),
        grid=(bsz, s // tm),
        in_specs=[
            tile,
            pl.BlockSpec((1, CONV_HALO, d), lambda b, i: (b, jnp.maximum(i * nh - 1, 0), 0)),
            pl.BlockSpec((1, CONV_HALO, d), lambda b, i: (b, jnp.minimum((i + 1) * nh, last), 0)),
            tile,
            pl.BlockSpec((CONV_KERNEL, d), lambda b, i: (0, 0)),
            vec, vec, vec,
            pl.BlockSpec((d, d), lambda b, i: (0, 0), pipeline_mode=pl.Buffered(1)),
            vec,
        ],
        out_specs=tile,
        scratch_shapes=[pltpu.VMEM((tm + 2 * CONV_HALO, d), F32), pltpu.VMEM((tm, d), F32),
                        pltpu.VMEM((SUBLANES - 1, tm + 2 * CONV_HALO - SUBLANES, _CONV_LANES), F32)],
        compiler_params=_cparams(("parallel", "parallel")),
        name="conv_module",
    )(u, u, u, h, wdw, bdw, ln_g, ln_b, w2, b2)


def _ffn_kernel(h_ref, nw_ref, wg_ref, wu_ref, wd_ref, fw_ref, o_ref, hn_ref, acc_ref, *, final_norm):
    j = pl.program_id(1)

    @pl.when(j == 0)
    def _():
        hn_ref[...] = _rms(h_ref[...], nw_ref[...]).astype(BF16)
        acc_ref[...] = jnp.zeros_like(acc_ref)

    hn = hn_ref[...]
    g = jnp.dot(hn, wg_ref[...], preferred_element_type=F32)
    u = jnp.dot(hn, wu_ref[...], preferred_element_type=F32)
    a = (g * _sigmoid(g) * u).astype(BF16)
    acc_ref[...] += jnp.dot(a, wd_ref[...], preferred_element_type=F32)

    @pl.when(j == pl.num_programs(1) - 1)
    def _():
        y = h_ref[...] + acc_ref[...]
        if final_norm:
            y = _rms(y, fw_ref[...])
        o_ref[...] = y


def _ffn(h, nw, w_gate_up, w_down, fw, *, layer, tm, th, final_norm):
    m, d = h.shape
    hid = w_down.shape[1]
    nb = hid // th
    vec = pl.BlockSpec((1, d), lambda i, j: (0, 0))
    return pl.pallas_call(
        functools.partial(_ffn_kernel, final_norm=final_norm),
        out_shape=jax.ShapeDtypeStruct((m, d), F32),
        grid=(m // tm, nb),
        in_specs=[
            pl.BlockSpec((tm, d), lambda i, j: (i, 0)),
            vec,
            pl.BlockSpec((None, d, th), lambda i, j: (layer, 0, j)),
            pl.BlockSpec((None, d, th), lambda i, j: (layer, 0, j + nb)),
            pl.BlockSpec((None, th, d), lambda i, j: (layer, j, 0)),
            vec,
        ],
        out_specs=pl.BlockSpec((tm, d), lambda i, j: (i, 0)),
        scratch_shapes=[pltpu.VMEM((tm, d), BF16), pltpu.VMEM((tm, d), F32)],
        compiler_params=_cparams(("parallel", "arbitrary")),
        name="ffn",
    )(h, nw, w_gate_up, w_gate_up, w_down, fw)


def _dn_prep_kernel(xc_ref, xp_ref, xn_ref, wc_ref, ba_ref, alog_ref, dtb_ref,
                    q_ref, k_ref, v_ref, gate_ref, ext_ref):
    i = pl.program_id(1)
    tm = xc_ref.shape[1]
    key_dim = q_ref.shape[2]
    pad = (SHORT_CONV - 1) // 2
    ext_ref[0:SHORT_HALO, :] = jnp.where(i > 0, xp_ref[0], 0.0)
    ext_ref[SHORT_HALO:SHORT_HALO + tm, :] = xc_ref[0]
    ext_ref[SHORT_HALO + tm:, :] = jnp.where(i < pl.num_programs(1) - 1, xn_ref[0], 0.0)

    base = SHORT_HALO - pad
    n_heads = xc_ref.shape[2] // HEAD_DIM
    for hd in range(n_heads):
        cols = slice(hd * HEAD_DIM, (hd + 1) * HEAD_DIM)
        acc = jnp.zeros((tm, HEAD_DIM), F32)
        for j in range(SHORT_CONV):
            acc = acc + wc_ref[j:j + 1, cols] * ext_ref[base + j:base + j + tm, cols]
        y = acc * _sigmoid(acc)
        c0 = hd * HEAD_DIM
        if c0 < 2 * key_dim:
            y = y * lax.rsqrt(jnp.sum(y * y, axis=-1, keepdims=True) + L2_EPS)
            if c0 < key_dim:
                q_ref[0, :, cols] = (y * (HEAD_DIM ** -0.5)).astype(BF16)
            else:
                k_ref[0, :, c0 - key_dim:c0 - key_dim + HEAD_DIM] = y.astype(BF16)
        else:
            v_ref[0, :, c0 - 2 * key_dim:c0 - 2 * key_dim + HEAD_DIM] = y.astype(BF16)

    raw = ba_ref[0]
    beta = _sigmoid(raw)
    z = raw + dtb_ref[...]
    softplus = jnp.maximum(z, 0.0) + jnp.log(1.0 + jnp.exp(-jnp.abs(z)))
    g = -jnp.exp(alog_ref[...]) * softplus
    r = lax.broadcasted_iota(jnp.int32, (2 * CHUNK, CHUNK), 0)
    c = lax.broadcasted_iota(jnp.int32, (2 * CHUNK, CHUNK), 1)
    tri = ((r < CHUNK) & (c <= r)) | ((r >= CHUNK) & (c >= r - CHUNK))
    tri = jnp.where(tri, 1.0, 0.0).astype(BF16)
    g_hi = g.astype(BF16)
    rest = g - g_hi.astype(F32)
    g_mid = rest.astype(BF16)
    g_lo = (rest - g_mid.astype(F32)).astype(BF16)
    quarter = HEAD_DIM // 4
    lane = lax.broadcasted_iota(jnp.int32, (CHUNK, HEAD_DIM), 1)
    for n in range(tm // CHUNK):
        rows = slice(n * CHUNK, (n + 1) * CHUNK)
        sums = (jnp.dot(tri, g_hi[rows], preferred_element_type=F32)
                + jnp.dot(tri, g_mid[rows], preferred_element_type=F32)
                + jnp.dot(tri, g_lo[rows], preferred_element_type=F32))
        gate_ref[0, rows, :] = jnp.where(lane < 2 * quarter, beta[rows],
                                         jnp.where(lane < 3 * quarter, sums[:CHUNK], sums[CHUNK:]))


def _dn_prep(proj, ba, w_conv, alog_pad, dtb_pad, *, tm, key_dim, value_dim):
    bsz, s, _ = proj.shape
    qkv_dim = 2 * key_dim + value_dim
    nh = tm // SHORT_HALO
    last = s // SHORT_HALO - 1
    return pl.pallas_call(
        _dn_prep_kernel,
        out_shape=[
            jax.ShapeDtypeStruct((bsz, s, key_dim), BF16),
            jax.ShapeDtypeStruct((bsz, s, key_dim), BF16),
            jax.ShapeDtypeStruct((bsz, s, value_dim), BF16),
            jax.ShapeDtypeStruct((bsz, s, HEAD_DIM), F32),
        ],
        grid=(bsz, s // tm),
        in_specs=[
            pl.BlockSpec((1, tm, qkv_dim), lambda b, i: (b, i, 0)),
            pl.BlockSpec((1, SHORT_HALO, qkv_dim), lambda b, i: (b, jnp.maximum(i * nh - 1, 0), 0)),
            pl.BlockSpec((1, SHORT_HALO, qkv_dim), lambda b, i: (b, jnp.minimum((i + 1) * nh, last), 0)),
            pl.BlockSpec((SHORT_CONV, qkv_dim), lambda b, i: (0, 0)),
            pl.BlockSpec((1, tm, HEAD_DIM), lambda b, i: (b, i, 0)),
            pl.BlockSpec((1, HEAD_DIM), lambda b, i: (0, 0)),
            pl.BlockSpec((1, HEAD_DIM), lambda b, i: (0, 0)),
        ],
        out_specs=[
            pl.BlockSpec((1, tm, key_dim), lambda b, i: (b, i, 0)),
            pl.BlockSpec((1, tm, key_dim), lambda b, i: (b, i, 0)),
            pl.BlockSpec((1, tm, value_dim), lambda b, i: (b, i, 0)),
            pl.BlockSpec((1, tm, HEAD_DIM), lambda b, i: (b, i, 0)),
        ],
        scratch_shapes=[pltpu.VMEM((tm + 2 * SHORT_HALO, qkv_dim), F32)],
        compiler_params=_cparams(("parallel", "parallel")),
        name="dn_prep",
    )(proj, proj, proj, w_conv, ba, alog_pad, dtb_pad)


N_CHAINS = 2 * V_PER_K
WIDE = N_CHAINS * CHUNK
PAIR = V_PER_K * HEAD_DIM
OPS_U = 2 * CHUNK
OPS_AK = 3 * CHUNK
OPS_ROWS = OPS_AK + HEAD_DIM
_INTRA_GROUP = 8


def _block_diag(x):
    n = x.shape[1]
    rb = lax.broadcasted_iota(jnp.int32, (n, n), 0) // CHUNK
    cb = lax.broadcasted_iota(jnp.int32, (n, n), 1) // CHUNK
    tiled = jnp.concatenate([x.astype(BF16)] * N_CHAINS, axis=0)
    return jnp.where(rb == cb, tiled, jnp.zeros_like(tiled))


def _delta_intra_kernel(q_ref, k_ref, v_ref, g_ref, grow_ref, ops_ref, dw_ref):
    kh = pl.program_id(1)
    cs = CHUNK
    quarter = HEAD_DIM // 4
    chunks = range(q_ref.shape[1] // cs)

    r = lax.broadcasted_iota(jnp.int32, (cs, WIDE), 0)
    lw = lax.broadcasted_iota(jnp.int32, (cs, WIDE), 1)
    blk = lw // cs
    c = lw - blk * cs
    ahead = jnp.where(blk < V_PER_K, r - c, c - r)
    incl = ahead >= 0
    strict = ahead > 0
    eye = jnp.where(r == c, 1.0, 0.0)
    lane_g = lax.broadcasted_iota(jnp.int32, (cs, HEAD_DIM), 1)

    def wide(cols):
        out = cols[N_CHAINS - 1]
        for j in range(N_CHAINS - 2, -1, -1):
            out = jnp.where(blk == j, cols[j], out)
        return out

    q, k, v, grow, beta_c, g_c, decay, p, t, a_w = ([None] * len(chunks) for _ in range(10))

    def prepare(n):
        rows = slice(n * cs, (n + 1) * cs)
        q[n], k[n], v[n], grow[n] = q_ref[0, rows, :], k_ref[0, rows, :], v_ref[0, rows, :], grow_ref[0, 0, n]
        gates = g_ref[0, rows, :]
        beta_c[n], g_c[n] = [], []
        for j in range(N_CHAINS):
            d, i = divmod(j, V_PER_K)
            head = kh * V_PER_K + i
            beta_c[n].append(jnp.sum(jnp.where(lane_g == d * quarter + head, gates, 0.0), axis=1, keepdims=True))
            g_c[n].append(jnp.sum(jnp.where(lane_g == (2 + d) * quarter + head, gates, 0.0), axis=1,
                                  keepdims=True))
        decay[n] = jnp.where(incl, jnp.exp(jnp.where(incl, wide(g_c[n]) - grow[n], 0.0)), 0.0)
        qkk = _mm_nt(jnp.concatenate([q[n], k[n]], axis=0), jnp.concatenate([k[n]] * N_CHAINS, axis=0))
        p[n] = -jnp.where(strict, qkk[cs:] * decay[n] * wide(beta_c[n]), 0.0)
        t[n] = eye + p[n]
        a_w[n] = jnp.where(incl, qkk[:cs] * decay[n], 0.0)

    def first_level(n):
        p[n] = _mm(p[n], _block_diag(p[n]))

    def middle_level(n):
        both = _mm(jnp.concatenate([p[n], t[n]], axis=0), _block_diag(p[n]))
        p[n] = both[:cs]
        t[n] = t[n] + both[cs:]

    def last_level(n):
        t[n] = t[n] + _mm(t[n], _block_diag(p[n]))

    levels = [first_level] + [middle_level] * (cs.bit_length() - 3) + [last_level]

    def finish(n):
        kf = k[n].astype(F32)
        qf = q[n].astype(F32)
        rhs, qg, kg, dwl = [], [], [], []
        for j in range(N_CHAINS):
            d, i = divmod(j, V_PER_K)
            eg = jnp.exp(g_c[n][j])
            vj = v[n][:, i * HEAD_DIM:(i + 1) * HEAD_DIM].astype(F32)
            rhs.append(jnp.concatenate([vj * beta_c[n][j], kf * (beta_c[n][j] * eg)], axis=1).astype(BF16))
            g_last = grow[n][:, j * cs:j * cs + 1] if d == 1 else grow[n][:, (j + 1) * cs - 1:(j + 1) * cs]
            qg.append(qf * eg)
            kg.append(kf * jnp.exp(g_last - g_c[n][j]))
            dwl.append(jnp.broadcast_to(jnp.exp(g_last), (1, HEAD_DIM)))
        lhs = jnp.concatenate([jnp.where(blk == j, t[n], 0.0) for j in range(N_CHAINS)], axis=0)
        uw = _mm(lhs, jnp.concatenate(rhs, axis=0))
        for d in range(2):
            j0, j1 = d * V_PER_K, d * V_PER_K + 1
            u0, w0 = uw[j0 * cs:(j0 + 1) * cs, :HEAD_DIM], uw[j0 * cs:(j0 + 1) * cs, HEAD_DIM:]
            u1, w1 = uw[j1 * cs:(j1 + 1) * cs, :HEAD_DIM], uw[j1 * cs:(j1 + 1) * cs, HEAD_DIM:]
            kgt = jnp.concatenate([kg[j0], kg[j1]], axis=0).T
            a_pad = jnp.concatenate([a_w[n][:, d * 2 * cs:(d + 1) * 2 * cs], jnp.zeros((HEAD_DIM - cs, 2 * cs), F32)],
                                    axis=0)
            ops = jnp.concatenate([jnp.concatenate([w0, w1], axis=1),
                                   jnp.concatenate([qg[j0], qg[j1]], axis=1),
                                   jnp.concatenate([u0, u1], axis=1),
                                   jnp.concatenate([a_pad, kgt], axis=1)], axis=0)
            ops_ref[d, 0, 0, n] = ops.astype(BF16)
            dw_ref[d, 0, 0, n:n + 1, :] = jnp.concatenate([dwl[j0], dwl[j1]], axis=1)

    groups = [list(chunks)[g:g + _INTRA_GROUP] for g in range(0, len(chunks), _INTRA_GROUP)]
    for stage in range(len(groups) + 2):
        inverse = groups[stage - 1] if 1 <= stage <= len(groups) else []
        filler = [functools.partial(finish, n) for n in (groups[stage - 2] if stage >= 2 else [])]
        filler += [functools.partial(prepare, n) for n in (groups[stage] if stage < len(groups) else [])]
        per_level = -(-len(filler) // len(levels))
        for li, level in enumerate(levels):
            for n in inverse:
                level(n)
            for task in filler[li * per_level:(li + 1) * per_level]:
                task()


def _delta_intra(q, k, v, gates, grow, *, chunks_per_step):
    bsz, s, key_dim = q.shape
    nkh = key_dim // HEAD_DIM
    nc = s // CHUNK
    tc = chunks_per_step
    tm = tc * CHUNK
    tok = lambda w: pl.BlockSpec((1, tm, w), lambda b, h, n: (b, n, h))
    return pl.pallas_call(
        _delta_intra_kernel,
        out_shape=[jax.ShapeDtypeStruct((2, bsz, nkh, nc, OPS_ROWS, PAIR), BF16),
                   jax.ShapeDtypeStruct((2, bsz, nkh, nc, PAIR), F32)],
        grid=(bsz, nkh, nc // tc),
        in_specs=[tok(HEAD_DIM), tok(HEAD_DIM), tok(PAIR),
                  pl.BlockSpec((1, tm, HEAD_DIM), lambda b, h, n: (b, n, 0)),
                  pl.BlockSpec((1, 1, tc, 1, WIDE), lambda b, h, n: (b, h, n, 0, 0))],
        out_specs=[pl.BlockSpec((2, 1, 1, tc, OPS_ROWS, PAIR), lambda b, h, n: (0, b, h, n, 0, 0)),
                   pl.BlockSpec((2, 1, 1, tc, PAIR), lambda b, h, n: (0, b, h, n, 0))],
        compiler_params=_cparams(("parallel", "parallel", "parallel")),
        name="delta_intra",
    )(q, k, v, gates, grow)


def _delta_state_kernel(opsf_ref, dwf_ref, opsb_ref, dwb_ref, of_ref, ob_ref, state_ref):
    c = pl.program_id(2)

    @pl.when(c == 0)
    def _():
        state_ref[...] = jnp.zeros_like(state_ref)

    hb = opsf_ref.shape[2]
    zeros_s = jnp.zeros((HEAD_DIM, HEAD_DIM), BF16)
    zeros_v = jnp.zeros((CHUNK, HEAD_DIM), BF16)
    work = []
    for d, (ops_ref, dw_ref, chunk) in enumerate(((opsf_ref, dwf_ref, c), (opsb_ref, dwb_ref, pl.num_programs(2) - 1 - c))):
        for h in range(hb):
            ak = jnp.concatenate([ops_ref[0, 0, h, 0, OPS_AK:OPS_AK + CHUNK, :HEAD_DIM],
                                  ops_ref[0, 0, h, 0, OPS_AK:, HEAD_DIM:]], axis=0)
            work.append((d, h, ops_ref[0, 0, h, 0, :OPS_U, :], ops_ref[0, 0, h, 0, OPS_U:OPS_AK, :], ak,
                         dw_ref[0, 0, h, pl.ds(chunk, 1), :], state_ref[d * hb + h]))
    ws_all = []
    for d, h, wq, u, ak, dw, state in work:
        sb = state.astype(BF16)
        s_bd = jnp.concatenate([jnp.concatenate([sb[:, :HEAD_DIM], zeros_s], axis=1),
                                jnp.concatenate([zeros_s, sb[:, HEAD_DIM:]], axis=1)], axis=0)
        ws_all.append(jnp.dot(wq, s_bd, preferred_element_type=F32))
    akv_all = []
    for (d, h, wq, u, ak, dw, state), ws in zip(work, ws_all):
        v_new = (u.astype(F32) - ws[:CHUNK]).astype(BF16)
        v_bd = jnp.concatenate([jnp.concatenate([v_new[:, :HEAD_DIM], zeros_v], axis=1),
                                jnp.concatenate([zeros_v, v_new[:, HEAD_DIM:]], axis=1)], axis=0)
        akv_all.append(jnp.dot(ak, v_bd, preferred_element_type=F32))
    for (d, h, wq, u, ak, dw, state), ws, akv in zip(work, ws_all, akv_all):
        o_ref = ob_ref if d else of_ref
        o_ref[0, :, h * PAIR:(h + 1) * PAIR] = (ws[CHUNK:] + akv[:CHUNK]).astype(o_ref.dtype)
        state_ref[d * hb + h] = state * dw + akv[CHUNK:]


def _delta_state(ops, dw, *, heads_per_step):
    _, bsz, nkh, nc = ops.shape[:4]
    hb = heads_per_step
    s = nc * CHUNK
    return pl.pallas_call(
        _delta_state_kernel,
        out_shape=[jax.ShapeDtypeStruct((bsz, s, nkh * PAIR), BF16)] * 2,
        grid=(bsz, nkh // hb, nc),
        in_specs=[pl.BlockSpec((1, 1, hb, 1, OPS_ROWS, PAIR), lambda b, h, c: (0, b, h, c, 0, 0)),
                  pl.BlockSpec((1, 1, hb, nc, PAIR), lambda b, h, c: (0, b, h, 0, 0)),
                  pl.BlockSpec((1, 1, hb, 1, OPS_ROWS, PAIR), lambda b, h, c: (1, b, h, nc - 1 - c, 0, 0)),
                  pl.BlockSpec((1, 1, hb, nc, PAIR), lambda b, h, c: (1, b, h, 0, 0))],
        out_specs=[pl.BlockSpec((1, CHUNK, hb * PAIR), lambda b, h, c: (b, c, h)),
                   pl.BlockSpec((1, CHUNK, hb * PAIR), lambda b, h, c: (b, nc - 1 - c, h))],
        scratch_shapes=[pltpu.VMEM((2 * hb, HEAD_DIM, PAIR), F32)],
        compiler_params=_cparams(("parallel", "parallel", "arbitrary")),
        name="delta_state",
    )(ops, dw, ops, dw)


def _delta_rule(q, k, v, gates):
    bsz, s, key_dim = q.shape
    nkh = key_dim // HEAD_DIM
    nc = s // CHUNK
    g = gates[:, :, HEAD_DIM // 2:].reshape(bsz, nc, CHUNK, 2, nkh, V_PER_K)
    grow = g.transpose(0, 4, 1, 3, 5, 2).reshape(bsz, nkh, nc, 1, WIDE)
    ops, dw = _delta_intra(q, k, v, gates, grow, chunks_per_step=16)
    return _delta_state(ops, dw, heads_per_step=16)


def _dn_out_kernel(of_ref, ob_ref, z_ref, nw_ref, w_ref, h_ref, o_ref, y_ref):
    for hd in range(of_ref.shape[1] // HEAD_DIM):
        cols = slice(hd * HEAD_DIM, (hd + 1) * HEAD_DIM)
        o = of_ref[:, cols].astype(F32) + ob_ref[:, cols].astype(F32)
        z = z_ref[:, cols]
        y = _rms(o, nw_ref[...]) * (z * _sigmoid(z))
        y_ref[:, cols] = y.astype(BF16)
    o_ref[...] = h_ref[...] + jnp.dot(y_ref[...], w_ref[...], preferred_element_type=F32)


def _dn_out(o_f, o_b, proj, z_block, norm_w, w_out, h, *, tm):
    m, vd = o_f.shape
    d = h.shape[1]
    return pl.pallas_call(
        _dn_out_kernel,
        out_shape=jax.ShapeDtypeStruct((m, d), F32),
        grid=(m // tm,),
        in_specs=[
            pl.BlockSpec((tm, vd), lambda i: (i, 0)),
            pl.BlockSpec((tm, vd), lambda i: (i, 0)),
            pl.BlockSpec((tm, vd), lambda i: (i, z_block)),
            pl.BlockSpec((1, HEAD_DIM), lambda i: (0, 0)),
            pl.BlockSpec((vd, d), lambda i: (0, 0), pipeline_mode=pl.Buffered(1)),
            pl.BlockSpec((tm, d), lambda i: (i, 0)),
        ],
        out_specs=pl.BlockSpec((tm, d), lambda i: (i, 0)),
        scratch_shapes=[pltpu.VMEM((tm, vd), BF16)],
        compiler_params=_cparams(("parallel",)),
        name="dn_out",
    )(o_f, o_b, proj, norm_w, w_out, h)


def _row(v):
    return v.reshape(1, -1).astype(F32)


def _conformer_layer(h, nw, w_pw1, b_pw1, w_dw, b_dw, ln_g, ln_b, w_pw2, b_pw2):
    bsz, s, d = h.shape
    u = _norm_glu(h.reshape(bsz * s, d), _row(nw), w_pw1.astype(BF16), _row(b_pw1), tm=1024, tn=512)
    return _conv_module(u.reshape(bsz, s, d), h, w_dw, _row(b_dw), _row(ln_g), _row(ln_b),
                        w_pw2.astype(BF16), _row(b_pw2), tm=256)


def _deltanet_layer(h, nw, w_in, w_conv, a_log, dt_bias, norm_w, w_out):
    bsz, s, d = h.shape
    key_dim = NUM_K_HEADS * HEAD_DIM
    value_dim = NUM_V_HEADS * HEAD_DIM
    main = 2 * key_dim + value_dim + value_dim
    h2 = h.reshape(bsz * s, d)
    proj, ba = _in_proj(h2, _row(nw), w_in.astype(BF16), main=main, tm=1024, tn=1024)
    zeros = jnp.zeros((2 * NUM_V_HEADS,), F32)
    alog_pad = jnp.concatenate([zeros, a_log.reshape(-1).astype(F32)]).reshape(1, -1)
    dtb_pad = jnp.concatenate([zeros, dt_bias.reshape(-1).astype(F32)]).reshape(1, -1)
    q, k, v, gates = _dn_prep(proj.reshape(bsz, s, main), ba.reshape(bsz, s, -1), w_conv, alog_pad, dtb_pad,
                              tm=256, key_dim=key_dim, value_dim=value_dim)
    o_f, o_b = _delta_rule(q, k, v, gates)
    out = _dn_out(o_f.reshape(bsz * s, value_dim), o_b.reshape(bsz * s, value_dim), proj,
                  (2 * key_dim + value_dim) // value_dim, _row(norm_w), w_out.astype(BF16), h2, tm=256)
    return out.reshape(bsz, s, d)


def kernel(x, mix_norm, ffn_norm, final_norm, cv_w_pw1, cv_b_pw1, cv_w_dw, cv_b_dw, cv_ln_g, cv_ln_b,
           cv_w_pw2, cv_b_pw2, dn_w_in, dn_w_conv, dn_a_log, dn_dt_bias, dn_norm_w, dn_w_out,
           ffn_w_gate_up, ffn_w_down):
    bsz, s, d = x.shape
    depth = mix_norm.shape[0]
    w_gate_up = ffn_w_gate_up.astype(BF16)
    w_down = ffn_w_down.astype(BF16)
    h = x
    for i in range(depth):
        j = i // 2
        if i % 2 == 0:
            h = _conformer_layer(h, mix_norm[i], cv_w_pw1[j], cv_b_pw1[j], cv_w_dw[j], cv_b_dw[j],
                                 cv_ln_g[j], cv_ln_b[j], cv_w_pw2[j], cv_b_pw2[j])
        else:
            h = _deltanet_layer(h, mix_norm[i], dn_w_in[j], dn_w_conv[j], dn_a_log[j], dn_dt_bias[j],
                                dn_norm_w[j], dn_w_out[j])
        last = i == depth - 1
        h = _ffn(h.reshape(bsz * s, d), _row(ffn_norm[i]), w_gate_up, w_down, _row(final_norm), layer=i,
                 tm=512, th=512, final_norm=last).reshape(bsz, s, d)
    return h
```

```python
import functools

import jax
import jax.numpy as jnp
from jax import lax
from jax.experimental import pallas as pl
from jax.experimental.pallas import tpu as pltpu

F32 = jnp.float32
BF16 = jnp.bfloat16

RMS_EPS = 1e-6
LN_EPS = 1e-5
L2_EPS = 1e-6

CONV_KERNEL = 31
CONV_HALO = 16
SHORT_CONV = 5
SHORT_HALO = 8

SUBLANES = 8
HEAD_DIM = 128
NUM_K_HEADS = 16
NUM_V_HEADS = 32
V_PER_K = NUM_V_HEADS // NUM_K_HEADS
CHUNK = 64

VMEM_LIMIT = 56 * 1024 * 1024


def _cparams(sem):
    return pltpu.CompilerParams(dimension_semantics=sem, vmem_limit_bytes=VMEM_LIMIT)


def _mm(a, b):
    return jnp.dot(a.astype(BF16), b.astype(BF16), preferred_element_type=F32)


def _mm_nt(a, b):
    return lax.dot_general(a.astype(BF16), b.astype(BF16), (((1,), (1,)), ((), ())),
                           preferred_element_type=F32)


def _rms(x, w):
    ms = jnp.mean(x * x, axis=-1, keepdims=True)
    return x * lax.rsqrt(ms + RMS_EPS) * w


def _sigmoid(x):
    return 1.0 / (1.0 + jnp.exp(-x))


def _in_proj_kernel(h_ref, nw_ref, w_ref, wt_ref, o_ref, tail_ref, hn_ref):
    @pl.when(pl.program_id(1) == 0)
    def _():
        hn = _rms(h_ref[...], nw_ref[...]).astype(BF16)
        hn_ref[...] = hn
        tail_ref[...] = jnp.dot(hn, wt_ref[...], preferred_element_type=F32)

    o_ref[...] = jnp.dot(hn_ref[...], w_ref[...], preferred_element_type=F32)


def _in_proj(h, nw, w, *, main, tm, tn):
    m, d = h.shape
    tail = w.shape[1] - main
    return pl.pallas_call(
        _in_proj_kernel,
        out_shape=[jax.ShapeDtypeStruct((m, main), F32), jax.ShapeDtypeStruct((m, tail), F32)],
        grid=(m // tm, main // tn),
        in_specs=[
            pl.BlockSpec((tm, d), lambda i, j: (i, 0)),
            pl.BlockSpec((1, d), lambda i, j: (0, 0)),
            pl.BlockSpec((d, tn), lambda i, j: (0, j)),
            pl.BlockSpec((d, tail), lambda i, j: (0, main // tail)),
        ],
        out_specs=[pl.BlockSpec((tm, tn), lambda i, j: (i, j)),
                   pl.BlockSpec((tm, tail), lambda i, j: (i, 0))],
        scratch_shapes=[pltpu.VMEM((tm, d), BF16)],
        compiler_params=_cparams(("parallel", "arbitrary")),
        name="in_proj",
    )(h, nw, w, w)


def _norm_glu_kernel(h_ref, nw_ref, wa_ref, wg_ref, ba_ref, bg_ref, o_ref, hn_ref):
    @pl.when(pl.program_id(1) == 0)
    def _():
        hn_ref[...] = _rms(h_ref[...], nw_ref[...]).astype(BF16)

    hn = hn_ref[...]
    a = jnp.dot(hn, wa_ref[...], preferred_element_type=F32) + ba_ref[...]
    g = jnp.dot(hn, wg_ref[...], preferred_element_type=F32) + bg_ref[...]
    o_ref[...] = a * _sigmoid(g)


def _norm_glu(h, nw, w, b, *, tm, tn):
    m, d = h.shape
    n = w.shape[1] // 2
    nb = n // tn
    return pl.pallas_call(
        _norm_glu_kernel,
        out_shape=jax.ShapeDtypeStruct((m, n), F32),
        grid=(m // tm, nb),
        in_specs=[
            pl.BlockSpec((tm, d), lambda i, j: (i, 0)),
            pl.BlockSpec((1, d), lambda i, j: (0, 0)),
            pl.BlockSpec((d, tn), lambda i, j: (0, j)),
            pl.BlockSpec((d, tn), lambda i, j: (0, j + nb)),
            pl.BlockSpec((1, tn), lambda i, j: (0, j)),
            pl.BlockSpec((1, tn), lambda i, j: (0, j + nb)),
        ],
        out_specs=pl.BlockSpec((tm, tn), lambda i, j: (i, j)),
        scratch_shapes=[pltpu.VMEM((tm, d), BF16)],
        compiler_params=_cparams(("parallel", "arbitrary")),
        name="norm_glu",
    )(h, nw, w, w, b, b)


_CONV_ROWS = 64
_CONV_LANES = 256


def _conv_module_kernel(uc_ref, up_ref, un_ref, h_ref, wdw_ref, bdw_ref, g_ref, b_ref, w2_ref, b2_ref,
                        o_ref, ext_ref, cv_ref, sh_ref):
    i = pl.program_id(1)
    tm = uc_ref.shape[1]
    d = uc_ref.shape[2]
    pad = (CONV_KERNEL - 1) // 2
    ext_ref[0:CONV_HALO, :] = jnp.where(i > 0, up_ref[0], 0.0)
    ext_ref[CONV_HALO:CONV_HALO + tm, :] = uc_ref[0]
    ext_ref[CONV_HALO + tm:, :] = jnp.where(i < pl.num_programs(1) - 1, un_ref[0], 0.0)

    base = CONV_HALO - pad
    span = sh_ref.shape[1]

    def lane_block(cb, carry):
        cols = pl.ds(pl.multiple_of(cb * _CONV_LANES, _CONV_LANES), _CONV_LANES)
        for s in range(1, SUBLANES):
            sh_ref[s - 1] = ext_ref[s:s + span, cols]
        for r0 in range(0, tm, _CONV_ROWS):
            acc = jnp.zeros((_CONV_ROWS, _CONV_LANES), F32)
            for j in range(CONV_KERNEL):
                s = (base + j) % SUBLANES
                a = r0 + base + j - s
                rows = ext_ref[a:a + _CONV_ROWS, cols] if s == 0 else sh_ref[s - 1, a:a + _CONV_ROWS, :]
                acc = acc + wdw_ref[j:j + 1, cols] * rows
            cv_ref[r0:r0 + _CONV_ROWS, cols] = acc + bdw_ref[:, cols]
        return carry

    lax.fori_loop(0, d // _CONV_LANES, lane_block, 0)

    x = cv_ref[...]
    mu = jnp.mean(x, axis=-1, keepdims=True)
    xc = x - mu
    var = jnp.mean(xc * xc, axis=-1, keepdims=True)
    y = xc * lax.rsqrt(var + LN_EPS) * g_ref[...] + b_ref[...]
    y = y * _sigmoid(y)
    o_ref[0] = h_ref[0] + jnp.dot(y.astype(BF16), w2_ref[...], preferred_element_type=F32) + b2_ref[...]


def _conv_module(u, h, wdw, bdw, ln_g, ln_b, w2, b2, *, tm):
    bsz, s, d = u.shape
    nh = tm // CONV_HALO
    last = s // CONV_HALO - 1
    tile = pl.BlockSpec((1, tm, d), lambda b, i: (b, i, 0))
    vec = pl.BlockSpec((1, d), lambda b, i: (0, 0))
    return pl.pallas_call(
        _conv_module_kernel,
        out_shape=jax.ShapeDtypeStruct((bsz, s, d), F32),
        grid=(bsz, s // tm),
        in_specs=[
            tile,
            pl.BlockSpec((1, CONV_HALO, d), lambda b, i: (b, jnp.maximum(i * nh - 1, 0), 0)),
            pl.BlockSpec((1, CONV_HALO, d), lambda b, i: (b, jnp.minimum((i + 1) * nh, last), 0)),
            tile,
            pl.BlockSpec((CONV_KERNEL, d), lambda b, i: (0, 0)),
            vec, vec, vec,
            pl.BlockSpec((d, d), lambda b, i: (0, 0), pipeline_mode=pl.Buffered(1)),
            vec,
        ],
        out_specs=tile,
        scratch_shapes=[pltpu.VMEM((tm + 2 * CONV_HALO, d), F32), pltpu.VMEM((tm, d), F32),
                        pltpu.VMEM((SUBLANES - 1, tm + 2 * CONV_HALO - SUBLANES, _CONV_LANES), F32)],
        compiler_params=_cparams(("parallel", "parallel")),
        name="conv_module",
    )(u, u, u, h, wdw, bdw, ln_g, ln_b, w2, b2)


def _ffn_kernel(h_ref, nw_ref, wg_ref, wu_ref, wd_ref, fw_ref, o_ref, hn_ref, acc_ref, *, final_norm):
    j = pl.program_id(1)

    @pl.when(j == 0)
    def _():
        hn_ref[...] = _rms(h_ref[...], nw_ref[...]).astype(BF16)
        acc_ref[...] = jnp.zeros_like(acc_ref)

    hn = hn_ref[...]
    g = jnp.dot(hn, wg_ref[...], preferred_element_type=F32)
    u = jnp.dot(hn, wu_ref[...], preferred_element_type=F32)
    a = (g * _sigmoid(g) * u).astype(BF16)
    acc_ref[...] += jnp.dot(a, wd_ref[...], preferred_element_type=F32)

    @pl.when(j == pl.num_programs(1) - 1)
    def _():
        y = h_ref[...] + acc_ref[...]
        if final_norm:
            y = _rms(y, fw_ref[...])
        o_ref[...] = y


def _ffn(h, nw, w_gate_up, w_down, fw, *, layer, tm, th, final_norm):
    m, d = h.shape
    hid = w_down.shape[1]
    nb = hid // th
    vec = pl.BlockSpec((1, d), lambda i, j: (0, 0))
    return pl.pallas_call(
        functools.partial(_ffn_kernel, final_norm=final_norm),
        out_shape=jax.ShapeDtypeStruct((m, d), F32),
        grid=(m // tm, nb),
        in_specs=[
            pl.BlockSpec((tm, d), lambda i, j: (i, 0)),
            vec,
            pl.BlockSpec((None, d, th), lambda i, j: (layer, 0, j)),
            pl.BlockSpec((None, d, th), lambda i, j: (layer, 0, j + nb)),
            pl.BlockSpec((None, th, d), lambda i, j: (layer, j, 0)),
            vec,
        ],
        out_specs=pl.BlockSpec((tm, d), lambda i, j: (i, 0)),
        scratch_shapes=[pltpu.VMEM((tm, d), BF16), pltpu.VMEM((tm, d), F32)],
        compiler_params=_cparams(("parallel", "arbitrary")),
        name="ffn",
    )(h, nw, w_gate_up, w_gate_up, w_down, fw)


def _dn_prep_kernel(xc_ref, xp_ref, xn_ref, wc_ref, ba_ref, alog_ref, dtb_ref,
                    q_ref, k_ref, v_ref, gate_ref, gate_t_ref, ext_ref):
    i = pl.program_id(1)
    tm = xc_ref.shape[1]
    key_dim = q_ref.shape[2]
    pad = (SHORT_CONV - 1) // 2
    ext_ref[0:SHORT_HALO, :] = jnp.where(i > 0, xp_ref[0], 0.0)
    ext_ref[SHORT_HALO:SHORT_HALO + tm, :] = xc_ref[0]
    ext_ref[SHORT_HALO + tm:, :] = jnp.where(i < pl.num_programs(1) - 1, xn_ref[0], 0.0)

    base = SHORT_HALO - pad
    n_heads = xc_ref.shape[2] // HEAD_DIM
    for hd in range(n_heads):
        cols = slice(hd * HEAD_DIM, (hd + 1) * HEAD_DIM)
        acc = jnp.zeros((tm, HEAD_DIM), F32)
        for j in range(SHORT_CONV):
            acc = acc + wc_ref[j:j + 1, cols] * ext_ref[base + j:base + j + tm, cols]
        y = acc * _sigmoid(acc)
        c0 = hd * HEAD_DIM
        if c0 < 2 * key_dim:
            y = y * lax.rsqrt(jnp.sum(y * y, axis=-1, keepdims=True) + L2_EPS)
            if c0 < key_dim:
                q_ref[0, :, cols] = (y * (HEAD_DIM ** -0.5)).astype(BF16)
            else:
                k_ref[0, :, c0 - key_dim:c0 - key_dim + HEAD_DIM] = y.astype(BF16)
        else:
            v_ref[0, :, c0 - 2 * key_dim:c0 - 2 * key_dim + HEAD_DIM] = y.astype(BF16)

    raw = ba_ref[0]
    beta = _sigmoid(raw)
    z = raw + dtb_ref[...]
    softplus = jnp.maximum(z, 0.0) + jnp.log(1.0 + jnp.exp(-jnp.abs(z)))
    g = -jnp.exp(alog_ref[...]) * softplus
    r = lax.broadcasted_iota(jnp.int32, (2 * CHUNK, CHUNK), 0)
    c = lax.broadcasted_iota(jnp.int32, (2 * CHUNK, CHUNK), 1)
    tri = ((r < CHUNK) & (c <= r)) | ((r >= CHUNK) & (c >= r - CHUNK))
    tri = jnp.where(tri, 1.0, 0.0).astype(BF16)
    g_hi = g.astype(BF16)
    rest = g - g_hi.astype(F32)
    g_mid = rest.astype(BF16)
    g_lo = (rest - g_mid.astype(F32)).astype(BF16)
    quarter = HEAD_DIM // 4
    lane = lax.broadcasted_iota(jnp.int32, (CHUNK, HEAD_DIM), 1)
    for n in range(tm // CHUNK):
        rows = slice(n * CHUNK, (n + 1) * CHUNK)
        sums = (jnp.dot(tri, g_hi[rows], preferred_element_type=F32)
                + jnp.dot(tri, g_mid[rows], preferred_element_type=F32)
                + jnp.dot(tri, g_lo[rows], preferred_element_type=F32))
        gate_ref[0, rows, :] = jnp.where(lane < 2 * quarter, beta[rows],
                                         jnp.where(lane < 3 * quarter, sums[:CHUNK], sums[CHUNK:]))
    for m in range(tm // HEAD_DIM):
        gate_t_ref[0, m] = gate_ref[0, m * HEAD_DIM:(m + 1) * HEAD_DIM, :].T


def _dn_prep(proj, ba, w_conv, alog_pad, dtb_pad, *, tm, key_dim, value_dim):
    bsz, s, _ = proj.shape
    qkv_dim = 2 * key_dim + value_dim
    nh = tm // SHORT_HALO
    last = s // SHORT_HALO - 1
    return pl.pallas_call(
        _dn_prep_kernel,
        out_shape=[
            jax.ShapeDtypeStruct((bsz, s, key_dim), BF16),
            jax.ShapeDtypeStruct((bsz, s, key_dim), BF16),
            jax.ShapeDtypeStruct((bsz, s, value_dim), BF16),
            jax.ShapeDtypeStruct((bsz, s, HEAD_DIM), F32),
            jax.ShapeDtypeStruct((bsz, s // HEAD_DIM, HEAD_DIM, HEAD_DIM), F32),
        ],
        grid=(bsz, s // tm),
        in_specs=[
            pl.BlockSpec((1, tm, qkv_dim), lambda b, i: (b, i, 0)),
            pl.BlockSpec((1, SHORT_HALO, qkv_dim), lambda b, i: (b, jnp.maximum(i * nh - 1, 0), 0)),
            pl.BlockSpec((1, SHORT_HALO, qkv_dim), lambda b, i: (b, jnp.minimum((i + 1) * nh, last), 0)),
            pl.BlockSpec((SHORT_CONV, qkv_dim), lambda b, i: (0, 0)),
            pl.BlockSpec((1, tm, HEAD_DIM), lambda b, i: (b, i, 0)),
            pl.BlockSpec((1, HEAD_DIM), lambda b, i: (0, 0)),
            pl.BlockSpec((1, HEAD_DIM), lambda b, i: (0, 0)),
        ],
        out_specs=[
            pl.BlockSpec((1, tm, key_dim), lambda b, i: (b, i, 0)),
            pl.BlockSpec((1, tm, key_dim), lambda b, i: (b, i, 0)),
            pl.BlockSpec((1, tm, value_dim), lambda b, i: (b, i, 0)),
            pl.BlockSpec((1, tm, HEAD_DIM), lambda b, i: (b, i, 0)),
            pl.BlockSpec((1, tm // HEAD_DIM, HEAD_DIM, HEAD_DIM), lambda b, i: (b, i, 0, 0)),
        ],
        scratch_shapes=[pltpu.VMEM((tm + 2 * SHORT_HALO, qkv_dim), F32)],
        compiler_params=_cparams(("parallel", "parallel")),
        name="dn_prep",
    )(proj, proj, proj, w_conv, ba, alog_pad, dtb_pad)


N_CHAINS = 2 * V_PER_K
WIDE = N_CHAINS * CHUNK
PAIR = V_PER_K * HEAD_DIM
OPS_U = 2 * CHUNK
OPS_AK = 3 * CHUNK
OPS_ROWS = OPS_AK + HEAD_DIM
_INTRA_GROUP = 8


def _block_diag(x):
    n = x.shape[1]
    rb = lax.broadcasted_iota(jnp.int32, (n, n), 0) // CHUNK
    cb = lax.broadcasted_iota(jnp.int32, (n, n), 1) // CHUNK
    tiled = jnp.concatenate([x.astype(BF16)] * N_CHAINS, axis=0)
    return jnp.where(rb == cb, tiled, jnp.zeros_like(tiled))


def _delta_intra_kernel(q_ref, k_ref, v_ref, g_ref, gt_ref, ops_ref, dw_ref):
    kh = pl.program_id(1)
    cs = CHUNK
    quarter = HEAD_DIM // 4
    chunks = range(q_ref.shape[1] // cs)

    r = lax.broadcasted_iota(jnp.int32, (cs, WIDE), 0)
    lw = lax.broadcasted_iota(jnp.int32, (cs, WIDE), 1)
    blk = lw // cs
    c = lw - blk * cs
    ahead = jnp.where(blk < V_PER_K, r - c, c - r)
    incl = ahead >= 0
    strict = ahead > 0
    eye = jnp.where(r == c, 1.0, 0.0)
    lane_g = lax.broadcasted_iota(jnp.int32, (cs, HEAD_DIM), 1)

    def wide(cols):
        out = cols[N_CHAINS - 1]
        for j in range(N_CHAINS - 2, -1, -1):
            out = jnp.where(blk == j, cols[j], out)
        return out

    q, k, v, grow, beta_c, g_c, decay, p, t, a_w = ([None] * len(chunks) for _ in range(10))
    g_rows = {}
    low_lanes = lax.broadcasted_iota(jnp.int32, (1, HEAD_DIM), 1) < cs

    def prepare(n):
        rows = slice(n * cs, (n + 1) * cs)
        q[n], k[n], v[n] = q_ref[0, rows, :], k_ref[0, rows, :], v_ref[0, rows, :]
        gates = g_ref[0, rows, :]
        pair, half = divmod(n, 2)
        if pair not in g_rows:
            rows4 = []
            for j in range(N_CHAINS):
                d, i = divmod(j, V_PER_K)
                rows4.append(gt_ref[0, pair, pl.ds((2 + d) * quarter + kh * V_PER_K + i, 1), :])
            g_rows[pair] = (rows4, [pltpu.roll(row, cs, axis=1) for row in rows4])
        plain, rolled = g_rows[pair]
        here, there = (plain, rolled) if half == 0 else (rolled, plain)
        grow[n] = jnp.concatenate([jnp.where(low_lanes, here[0], there[1]),
                                   jnp.where(low_lanes, here[2], there[3])], axis=1)
        beta_c[n], g_c[n] = [], []
        for j in range(N_CHAINS):
            d, i = divmod(j, V_PER_K)
            head = kh * V_PER_K + i
            beta_c[n].append(jnp.sum(jnp.where(lane_g == d * quarter + head, gates, 0.0), axis=1, keepdims=True))
            g_c[n].append(jnp.sum(jnp.where(lane_g == (2 + d) * quarter + head, gates, 0.0), axis=1,
                                  keepdims=True))
        decay[n] = jnp.where(incl, jnp.exp(jnp.where(incl, wide(g_c[n]) - grow[n], 0.0)), 0.0)
        qkk = _mm_nt(jnp.concatenate([q[n], k[n]], axis=0), jnp.concatenate([k[n]] * N_CHAINS, axis=0))
        p[n] = -jnp.where(strict, qkk[cs:] * decay[n] * wide(beta_c[n]), 0.0)
        t[n] = eye + p[n]
        a_w[n] = jnp.where(incl, qkk[:cs] * decay[n], 0.0)

    def first_level(n):
        p[n] = _mm(p[n], _block_diag(p[n]))

    def middle_level(n):
        both = _mm(jnp.concatenate([p[n], t[n]], axis=0), _block_diag(p[n]))
        p[n] = both[:cs]
        t[n] = t[n] + both[cs:]

    def last_level(n):
        t[n] = t[n] + _mm(t[n], _block_diag(p[n]))

    levels = [first_level] + [middle_level] * (cs.bit_length() - 3) + [last_level]

    def finish(n):
        kf = k[n].astype(F32)
        qf = q[n].astype(F32)
        rhs, qg, kg, dwl = [], [], [], []
        for j in range(N_CHAINS):
            d, i = divmod(j, V_PER_K)
            eg = jnp.exp(g_c[n][j])
            vj = v[n][:, i * HEAD_DIM:(i + 1) * HEAD_DIM].astype(F32)
            rhs.append(jnp.concatenate([vj * beta_c[n][j], kf * (beta_c[n][j] * eg)], axis=1).astype(BF16))
            g_last = grow[n][:, j * cs:j * cs + 1] if d == 1 else grow[n][:, (j + 1) * cs - 1:(j + 1) * cs]
            qg.append(qf * eg)
            kg.append(kf * jnp.exp(g_last - g_c[n][j]))
            dwl.append(jnp.broadcast_to(jnp.exp(g_last), (1, HEAD_DIM)))
        lhs = jnp.concatenate([jnp.where(blk == j, t[n], 0.0) for j in range(N_CHAINS)], axis=0)
        uw = _mm(lhs, jnp.concatenate(rhs, axis=0))
        for d in range(2):
            j0, j1 = d * V_PER_K, d * V_PER_K + 1
            u0, w0 = uw[j0 * cs:(j0 + 1) * cs, :HEAD_DIM], uw[j0 * cs:(j0 + 1) * cs, HEAD_DIM:]
            u1, w1 = uw[j1 * cs:(j1 + 1) * cs, :HEAD_DIM], uw[j1 * cs:(j1 + 1) * cs, HEAD_DIM:]
            kgt = jnp.concatenate([kg[j0], kg[j1]], axis=0).T
            a_pad = jnp.concatenate([a_w[n][:, d * 2 * cs:(d + 1) * 2 * cs], jnp.zeros((HEAD_DIM - cs, 2 * cs), F32)],
                                    axis=0)
            ops = jnp.concatenate([jnp.concatenate([w0, w1], axis=1),
                                   jnp.concatenate([qg[j0], qg[j1]], axis=1),
                                   jnp.concatenate([u0, u1], axis=1),
                                   jnp.concatenate([a_pad, kgt], axis=1)], axis=0)
            ops_ref[d, 0, 0, n] = ops.astype(BF16)
            dw_ref[d, 0, 0, n:n + 1, :] = jnp.concatenate([dwl[j0], dwl[j1]], axis=1)

    groups = [list(chunks)[g:g + _INTRA_GROUP] for g in range(0, len(chunks), _INTRA_GROUP)]
    for stage in range(len(groups) + 2):
        inverse = groups[stage - 1] if 1 <= stage <= len(groups) else []
        filler = [functools.partial(finish, n) for n in (groups[stage - 2] if stage >= 2 else [])]
        filler += [functools.partial(prepare, n) for n in (groups[stage] if stage < len(groups) else [])]
        per_level = -(-len(filler) // len(levels))
        for li, level in enumerate(levels):
            for n in inverse:
                level(n)
            for task in filler[li * per_level:(li + 1) * per_level]:
                task()


def _delta_intra(q, k, v, gates, gates_t, *, chunks_per_step):
    bsz, s, key_dim = q.shape
    nkh = key_dim // HEAD_DIM
    nc = s // CHUNK
    tc = chunks_per_step
    tm = tc * CHUNK
    tok = lambda w: pl.BlockSpec((1, tm, w), lambda b, h, n: (b, n, h))
    return pl.pallas_call(
        _delta_intra_kernel,
        out_shape=[jax.ShapeDtypeStruct((2, bsz, nkh, nc, OPS_ROWS, PAIR), BF16),
                   jax.ShapeDtypeStruct((2, bsz, nkh, nc, PAIR), F32)],
        grid=(bsz, nkh, nc // tc),
        in_specs=[tok(HEAD_DIM), tok(HEAD_DIM), tok(PAIR),
                  pl.BlockSpec((1, tm, HEAD_DIM), lambda b, h, n: (b, n, 0)),
                  pl.BlockSpec((1, tm // HEAD_DIM, HEAD_DIM, HEAD_DIM), lambda b, h, n: (b, n, 0, 0))],
        out_specs=[pl.BlockSpec((2, 1, 1, tc, OPS_ROWS, PAIR), lambda b, h, n: (0, b, h, n, 0, 0)),
                   pl.BlockSpec((2, 1, 1, tc, PAIR), lambda b, h, n: (0, b, h, n, 0))],
        compiler_params=_cparams(("parallel", "parallel", "parallel")),
        name="delta_intra",
    )(q, k, v, gates, gates_t)


def _delta_state_kernel(opsf_ref, dwf_ref, opsb_ref, dwb_ref, of_ref, ob_ref, state_ref):
    c = pl.program_id(2)

    @pl.when(c == 0)
    def _():
        state_ref[...] = jnp.zeros_like(state_ref)

    hb = opsf_ref.shape[2]
    zeros_s = jnp.zeros((HEAD_DIM, HEAD_DIM), BF16)
    zeros_v = jnp.zeros((CHUNK, HEAD_DIM), BF16)
    work = []
    for d, (ops_ref, dw_ref, chunk) in enumerate(((opsf_ref, dwf_ref, c), (opsb_ref, dwb_ref, pl.num_programs(2) - 1 - c))):
        for h in range(hb):
            ak = jnp.concatenate([ops_ref[0, 0, h, 0, OPS_AK:OPS_AK + CHUNK, :HEAD_DIM],
                                  ops_ref[0, 0, h, 0, OPS_AK:, HEAD_DIM:]], axis=0)
            work.append((d, h, ops_ref[0, 0, h, 0, :OPS_U, :], ops_ref[0, 0, h, 0, OPS_U:OPS_AK, :], ak,
                         dw_ref[0, 0, h, pl.ds(chunk, 1), :], state_ref[d * hb + h]))
    ws_all = []
    for d, h, wq, u, ak, dw, state in work:
        sb = state.astype(BF16)
        s_bd = jnp.concatenate([jnp.concatenate([sb[:, :HEAD_DIM], zeros_s], axis=1),
                                jnp.concatenate([zeros_s, sb[:, HEAD_DIM:]], axis=1)], axis=0)
        ws_all.append(jnp.dot(wq, s_bd, preferred_element_type=F32))
    akv_all = []
    for (d, h, wq, u, ak, dw, state), ws in zip(work, ws_all):
        v_new = (u.astype(F32) - ws[:CHUNK]).astype(BF16)
        v_bd = jnp.concatenate([jnp.concatenate([v_new[:, :HEAD_DIM], zeros_v], axis=1),
                                jnp.concatenate([zeros_v, v_new[:, HEAD_DIM:]], axis=1)], axis=0)
        akv_all.append(jnp.dot(ak, v_bd, preferred_element_type=F32))
    for (d, h, wq, u, ak, dw, state), ws, akv in zip(work, ws_all, akv_all):
        o_ref = ob_ref if d else of_ref
        o_ref[0, :, h * PAIR:(h + 1) * PAIR] = (ws[CHUNK:] + akv[:CHUNK]).astype(o_ref.dtype)
        state_ref[d * hb + h] = state * dw + akv[CHUNK:]


def _delta_state(ops, dw, *, heads_per_step):
    _, bsz, nkh, nc = ops.shape[:4]
    hb = heads_per_step
    s = nc * CHUNK
    return pl.pallas_call(
        _delta_state_kernel,
        out_shape=[jax.ShapeDtypeStruct((bsz, s, nkh * PAIR), BF16)] * 2,
        grid=(bsz, nkh // hb, nc),
        in_specs=[pl.BlockSpec((1, 1, hb, 1, OPS_ROWS, PAIR), lambda b, h, c: (0, b, h, c, 0, 0)),
                  pl.BlockSpec((1, 1, hb, nc, PAIR), lambda b, h, c: (0, b, h, 0, 0)),
                  pl.BlockSpec((1, 1, hb, 1, OPS_ROWS, PAIR), lambda b, h, c: (1, b, h, nc - 1 - c, 0, 0)),
                  pl.BlockSpec((1, 1, hb, nc, PAIR), lambda b, h, c: (1, b, h, 0, 0))],
        out_specs=[pl.BlockSpec((1, CHUNK, hb * PAIR), lambda b, h, c: (b, c, h)),
                   pl.BlockSpec((1, CHUNK, hb * PAIR), lambda b, h, c: (b, nc - 1 - c, h))],
        scratch_shapes=[pltpu.VMEM((2 * hb, HEAD_DIM, PAIR), F32)],
        compiler_params=_cparams(("parallel", "parallel", "arbitrary")),
        name="delta_state",
    )(ops, dw, ops, dw)


def _delta_rule(q, k, v, gates, gates_t):
    ops, dw = _delta_intra(q, k, v, gates, gates_t, chunks_per_step=32)
    return _delta_state(ops, dw, heads_per_step=16)


def _dn_out_kernel(of_ref, ob_ref, z_ref, nw_ref, w_ref, h_ref, o_ref, y_ref):
    for hd in range(of_ref.shape[1] // HEAD_DIM):
        cols = slice(hd * HEAD_DIM, (hd + 1) * HEAD_DIM)
        o = of_ref[:, cols].astype(F32) + ob_ref[:, cols].astype(F32)
        z = z_ref[:, cols]
        y = _rms(o, nw_ref[...]) * (z * _sigmoid(z))
        y_ref[:, cols] = y.astype(BF16)
    o_ref[...] = h_ref[...] + jnp.dot(y_ref[...], w_ref[...], preferred_element_type=F32)


def _dn_out(o_f, o_b, proj, z_block, norm_w, w_out, h, *, tm):
    m, vd = o_f.shape
    d = h.shape[1]
    return pl.pallas_call(
        _dn_out_kernel,
        out_shape=jax.ShapeDtypeStruct((m, d), F32),
        grid=(m // tm,),
        in_specs=[
            pl.BlockSpec((tm, vd), lambda i: (i, 0)),
            pl.BlockSpec((tm, vd), lambda i: (i, 0)),
            pl.BlockSpec((tm, vd), lambda i: (i, z_block)),
            pl.BlockSpec((1, HEAD_DIM), lambda i: (0, 0)),
            pl.BlockSpec((vd, d), lambda i: (0, 0), pipeline_mode=pl.Buffered(1)),
            pl.BlockSpec((tm, d), lambda i: (i, 0)),
        ],
        out_specs=pl.BlockSpec((tm, d), lambda i: (i, 0)),
        scratch_shapes=[pltpu.VMEM((tm, vd), BF16)],
        compiler_params=_cparams(("parallel",)),
        name="dn_out",
    )(o_f, o_b, proj, norm_w, w_out, h)


def _row(v):
    return v.reshape(1, -1).astype(F32)


def _conformer_layer(h, nw, w_pw1, b_pw1, w_dw, b_dw, ln_g, ln_b, w_pw2, b_pw2):
    bsz, s, d = h.shape
    u = _norm_glu(h.reshape(bsz * s, d), _row(nw), w_pw1.astype(BF16), _row(b_pw1), tm=1024, tn=512)
    return _conv_module(u.reshape(bsz, s, d), h, w_dw, _row(b_dw), _row(ln_g), _row(ln_b),
                        w_pw2.astype(BF16), _row(b_pw2), tm=256)


def _deltanet_layer(h, nw, w_in, w_conv, a_log, dt_bias, norm_w, w_out):
    bsz, s, d = h.shape
    key_dim = NUM_K_HEADS * HEAD_DIM
    value_dim = NUM_V_HEADS * HEAD_DIM
    main = 2 * key_dim + value_dim + value_dim
    h2 = h.reshape(bsz * s, d)
    proj, ba = _in_proj(h2, _row(nw), w_in.astype(BF16), main=main, tm=1024, tn=1024)
    zeros = jnp.zeros((2 * NUM_V_HEADS,), F32)
    alog_pad = jnp.concatenate([zeros, a_log.reshape(-1).astype(F32)]).reshape(1, -1)
    dtb_pad = jnp.concatenate([zeros, dt_bias.reshape(-1).astype(F32)]).reshape(1, -1)
    q, k, v, gates, gates_t = _dn_prep(proj.reshape(bsz, s, main), ba.reshape(bsz, s, -1), w_conv, alog_pad,
                                       dtb_pad, tm=256, key_dim=key_dim, value_dim=value_dim)
    o_f, o_b = _delta_rule(q, k, v, gates, gates_t)
    out = _dn_out(o_f.reshape(bsz * s, value_dim), o_b.reshape(bsz * s, value_dim), proj,
                  (2 * key_dim + value_dim) // value_dim, _row(norm_w), w_out.astype(BF16), h2, tm=256)
    return out.reshape(bsz, s, d)


def kernel(x, mix_norm, ffn_norm, final_norm, cv_w_pw1, cv_b_pw1, cv_w_dw, cv_b_dw, cv_ln_g, cv_ln_b,
           cv_w_pw2, cv_b_pw2, dn_w_in, dn_w_conv, dn_a_log, dn_dt_bias, dn_norm_w, dn_w_out,
           ffn_w_gate_up, ffn_w_down):
    bsz, s, d = x.shape
    depth = mix_norm.shape[0]
    w_gate_up = ffn_w_gate_up.astype(BF16)
    w_down = ffn_w_down.astype(BF16)
    h = x
    for i in range(depth):
        j = i // 2
        if i % 2 == 0:
            h = _conformer_layer(h, mix_norm[i], cv_w_pw1[j], cv_b_pw1[j], cv_w_dw[j], cv_b_dw[j],
                                 cv_ln_g[j], cv_ln_b[j], cv_w_pw2[j], cv_b_pw2[j])
        else:
            h = _deltanet_layer(h, mix_norm[i], dn_w_in[j], dn_w_conv[j], dn_a_log[j], dn_dt_bias[j],
                                dn_norm_w[j], dn_w_out[j])
        last = i == depth - 1
        h = _ffn(h.reshape(bsz * s, d), _row(ffn_norm[i]), w_gate_up, w_down, _row(final_norm), layer=i,
                 tm=512, th=512, final_norm=last).reshape(bsz, s, d)
    return h
```

```python
import functools

import jax
import jax.numpy as jnp
from jax import lax
from jax.experimental import pallas as pl
from jax.experimental.pallas import tpu as pltpu

F32 = jnp.float32
BF16 = jnp.bfloat16

RMS_EPS = 1e-6
LN_EPS = 1e-5
L2_EPS = 1e-6

CONV_KERNEL = 31
CONV_HALO = 16
SHORT_CONV = 5
SHORT_HALO = 8

SUBLANES = 8
HEAD_DIM = 128
NUM_K_HEADS = 16
NUM_V_HEADS = 32
V_PER_K = NUM_V_HEADS // NUM_K_HEADS
CHUNK = 64

VMEM_LIMIT = 56 * 1024 * 1024


def _cparams(sem):
    return pltpu.CompilerParams(dimension_semantics=sem, vmem_limit_bytes=VMEM_LIMIT)


def _mm(a, b):
    return jnp.dot(a.astype(BF16), b.astype(BF16), preferred_element_type=F32)


def _mm_nt(a, b):
    return lax.dot_general(a.astype(BF16), b.astype(BF16), (((1,), (1,)), ((), ())),
                           preferred_element_type=F32)


def _rms(x, w):
    ms = jnp.mean(x * x, axis=-1, keepdims=True)
    return x * lax.rsqrt(ms + RMS_EPS) * w


def _sigmoid(x):
    return 1.0 / (1.0 + jnp.exp(-x))


def _in_proj_kernel(h_ref, nw_ref, w_ref, wt_ref, o_ref, tail_ref, hn_ref):
    @pl.when(pl.program_id(1) == 0)
    def _():
        hn = _rms(h_ref[...], nw_ref[...]).astype(BF16)
        hn_ref[...] = hn
        tail_ref[...] = jnp.dot(hn, wt_ref[...], preferred_element_type=F32)

    o_ref[...] = jnp.dot(hn_ref[...], w_ref[...], preferred_element_type=F32)


def _in_proj(h, nw, w, *, main, tm, tn):
    m, d = h.shape
    tail = w.shape[1] - main
    return pl.pallas_call(
        _in_proj_kernel,
        out_shape=[jax.ShapeDtypeStruct((m, main), F32), jax.ShapeDtypeStruct((m, tail), F32)],
        grid=(m // tm, main // tn),
        in_specs=[
            pl.BlockSpec((tm, d), lambda i, j: (i, 0)),
            pl.BlockSpec((1, d), lambda i, j: (0, 0)),
            pl.BlockSpec((d, tn), lambda i, j: (0, j)),
            pl.BlockSpec((d, tail), lambda i, j: (0, main // tail)),
        ],
        out_specs=[pl.BlockSpec((tm, tn), lambda i, j: (i, j)),
                   pl.BlockSpec((tm, tail), lambda i, j: (i, 0))],
        scratch_shapes=[pltpu.VMEM((tm, d), BF16)],
        compiler_params=_cparams(("parallel", "arbitrary")),
        name="in_proj",
    )(h, nw, w, w)


def _norm_glu_kernel(h_ref, nw_ref, wa_ref, wg_ref, ba_ref, bg_ref, o_ref, hn_ref):
    @pl.when(pl.program_id(1) == 0)
    def _():
        hn_ref[...] = _rms(h_ref[...], nw_ref[...]).astype(BF16)

    hn = hn_ref[...]
    a = jnp.dot(hn, wa_ref[...], preferred_element_type=F32) + ba_ref[...]
    g = jnp.dot(hn, wg_ref[...], preferred_element_type=F32) + bg_ref[...]
    o_ref[...] = a * _sigmoid(g)


def _norm_glu(h, nw, w, b, *, tm, tn):
    m, d = h.shape
    n = w.shape[1] // 2
    nb = n // tn
    return pl.pallas_call(
        _norm_glu_kernel,
        out_shape=jax.ShapeDtypeStruct((m, n), F32),
        grid=(m // tm, nb),
        in_specs=[
            pl.BlockSpec((tm, d), lambda i, j: (i, 0)),
            pl.BlockSpec((1, d), lambda i, j: (0, 0)),
            pl.BlockSpec((d, tn), lambda i, j: (0, j)),
            pl.BlockSpec((d, tn), lambda i, j: (0, j + nb)),
            pl.BlockSpec((1, tn), lambda i, j: (0, j)),
            pl.BlockSpec((1, tn), lambda i, j: (0, j + nb)),
        ],
        out_specs=pl.BlockSpec((tm, tn), lambda i, j: (i, j)),
        scratch_shapes=[pltpu.VMEM((tm, d), BF16)],
        compiler_params=_cparams(("parallel", "arbitrary")),
        name="norm_glu",
    )(h, nw, w, w, b, b)


_CONV_ROWS = 64
_CONV_LANES = 256


def _conv_module_kernel(uc_ref, up_ref, un_ref, h_ref, wdw_ref, bdw_ref, g_ref, b_ref, w2_ref, b2_ref,
                        o_ref, ext_ref, cv_ref, sh_ref):
    i = pl.program_id(1)
    tm = uc_ref.shape[1]
    d = uc_ref.shape[2]
    pad = (CONV_KERNEL - 1) // 2
    ext_ref[0:CONV_HALO, :] = jnp.where(i > 0, up_ref[0], 0.0)
    ext_ref[CONV_HALO:CONV_HALO + tm, :] = uc_ref[0]
    ext_ref[CONV_HALO + tm:, :] = jnp.where(i < pl.num_programs(1) - 1, un_ref[0], 0.0)

    base = CONV_HALO - pad
    span = sh_ref.shape[1]

    def lane_block(cb, carry):
        cols = pl.ds(pl.multiple_of(cb * _CONV_LANES, _CONV_LANES), _CONV_LANES)
        for s in range(1, SUBLANES):
            sh_ref[s - 1] = ext_ref[s:s + span, cols]
        for r0 in range(0, tm, _CONV_ROWS):
            acc = jnp.zeros((_CONV_ROWS, _CONV_LANES), F32)
            for j in range(CONV_KERNEL):
                s = (base + j) % SUBLANES
                a = r0 + base + j - s
                rows = ext_ref[a:a + _CONV_ROWS, cols] if s == 0 else sh_ref[s - 1, a:a + _CONV_ROWS, :]
                acc = acc + wdw_ref[j:j + 1, cols] * rows
            cv_ref[r0:r0 + _CONV_ROWS, cols] = acc + bdw_ref[:, cols]
        return carry

    lax.fori_loop(0, d // _CONV_LANES, lane_block, 0)

    x = cv_ref[...]
    mu = jnp.mean(x, axis=-1, keepdims=True)
    xc = x - mu
    var = jnp.mean(xc * xc, axis=-1, keepdims=True)
    y = xc * lax.rsqrt(var + LN_EPS) * g_ref[...] + b_ref[...]
    y = y * _sigmoid(y)
    o_ref[0] = h_ref[0] + jnp.dot(y.astype(BF16), w2_ref[...], preferred_element_type=F32) + b2_ref[...]


def _conv_module(u, h, wdw, bdw, ln_g, ln_b, w2, b2, *, tm):
    bsz, s, d = u.shape
    nh = tm // CONV_HALO
    last = s // CONV_HALO - 1
    tile = pl.BlockSpec((1, tm, d), lambda b, i: (b, i, 0))
    vec = pl.BlockSpec((1, d), lambda b, i: (0, 0))
    return pl.pallas_call(
        _conv_module_kernel,
        out_shape=jax.ShapeDtypeStruct((bsz, s, d), F32),
        grid=(bsz, s // tm),
        in_specs=[
            tile,
            pl.BlockSpec((1, CONV_HALO, d), lambda b, i: (b, jnp.maximum(i * nh - 1, 0), 0)),
            pl.BlockSpec((1, CONV_HALO, d), lambda b, i: (b, jnp.minimum((i + 1) * nh, last), 0)),
            tile,
            pl.BlockSpec((CONV_KERNEL, d), lambda b, i: (0, 0)),
            vec, vec, vec,
            pl.BlockSpec((d, d), lambda b, i: (0, 0), pipeline_mode=pl.Buffered(1)),
            vec,
        ],
        out_specs=tile,
        scratch_shapes=[pltpu.VMEM((tm + 2 * CONV_HALO, d), F32), pltpu.VMEM((tm, d), F32),
                        pltpu.VMEM((SUBLANES - 1, tm + 2 * CONV_HALO - SUBLANES, _CONV_LANES), F32)],
        compiler_params=_cparams(("parallel", "parallel")),
        name="conv_module",
    )(u, u, u, h, wdw, bdw, ln_g, ln_b, w2, b2)


def _ffn_kernel(h_ref, nw_ref, wg_ref, wu_ref, wd_ref, fw_ref, o_ref, hn_ref, acc_ref, *, final_norm):
    j = pl.program_id(1)

    @pl.when(j == 0)
    def _():
        hn_ref[...] = _rms(h_ref[...], nw_ref[...]).astype(BF16)
        acc_ref[...] = jnp.zeros_like(acc_ref)

    hn = hn_ref[...]
    g = jnp.dot(hn, wg_ref[...], preferred_element_type=F32)
    u = jnp.dot(hn, wu_ref[...], preferred_element_type=F32)
    a = (g * _sigmoid(g) * u).astype(BF16)
    acc_ref[...] += jnp.dot(a, wd_ref[...], preferred_element_type=F32)

    @pl.when(j == pl.num_programs(1) - 1)
    def _():
        y = h_ref[...] + acc_ref[...]
        if final_norm:
            y = _rms(y, fw_ref[...])
        o_ref[...] = y


def _ffn(h, nw, w_gate_up, w_down, fw, *, layer, tm, th, final_norm):
    m, d = h.shape
    hid = w_down.shape[1]
    nb = hid // th
    vec = pl.BlockSpec((1, d), lambda i, j: (0, 0))
    return pl.pallas_call(
        functools.partial(_ffn_kernel, final_norm=final_norm),
        out_shape=jax.ShapeDtypeStruct((m, d), F32),
        grid=(m // tm, nb),
        in_specs=[
            pl.BlockSpec((tm, d), lambda i, j: (i, 0)),
            vec,
            pl.BlockSpec((None, d, th), lambda i, j: (layer, 0, j)),
            pl.BlockSpec((None, d, th), lambda i, j: (layer, 0, j + nb)),
            pl.BlockSpec((None, th, d), lambda i, j: (layer, j, 0)),
            vec,
        ],
        out_specs=pl.BlockSpec((tm, d), lambda i, j: (i, 0)),
        scratch_shapes=[pltpu.VMEM((tm, d), BF16), pltpu.VMEM((tm, d), F32)],
        compiler_params=_cparams(("parallel", "arbitrary")),
        name="ffn",
    )(h, nw, w_gate_up, w_gate_up, w_down, fw)


def _dn_prep_kernel(xc_ref, xp_ref, xn_ref, wc_ref, ba_ref, alog_ref, dtb_ref,
                    q_ref, k_ref, v_ref, gate_ref, gate_t_ref, ext_ref):
    i = pl.program_id(1)
    tm = xc_ref.shape[1]
    key_dim = q_ref.shape[2]
    pad = (SHORT_CONV - 1) // 2
    ext_ref[0:SHORT_HALO, :] = jnp.where(i > 0, xp_ref[0], 0.0)
    ext_ref[SHORT_HALO:SHORT_HALO + tm, :] = xc_ref[0]
    ext_ref[SHORT_HALO + tm:, :] = jnp.where(i < pl.num_programs(1) - 1, xn_ref[0], 0.0)

    base = SHORT_HALO - pad
    n_heads = xc_ref.shape[2] // HEAD_DIM
    for hd in range(n_heads):
        cols = slice(hd * HEAD_DIM, (hd + 1) * HEAD_DIM)
        taps = sorted(range(SHORT_CONV), key=lambda j: (base + j) % SUBLANES)
        acc = wc_ref[taps[0]:taps[0] + 1, cols] * ext_ref[base + taps[0]:base + taps[0] + tm, cols]
        for j in taps[1:]:
            acc = acc + wc_ref[j:j + 1, cols] * ext_ref[base + j:base + j + tm, cols]
        y = acc * _sigmoid(acc)
        c0 = hd * HEAD_DIM
        if c0 < 2 * key_dim:
            inv = lax.rsqrt(jnp.sum(y * y, axis=-1, keepdims=True) + L2_EPS)
            if c0 < key_dim:
                q_ref[0, :, cols] = (y * (inv * (HEAD_DIM ** -0.5))).astype(BF16)
            else:
                k_ref[0, :, c0 - key_dim:c0 - key_dim + HEAD_DIM] = (y * inv).astype(BF16)
        else:
            v_ref[0, :, c0 - 2 * key_dim:c0 - 2 * key_dim + HEAD_DIM] = y.astype(BF16)

    raw = ba_ref[0]
    beta = _sigmoid(raw)
    z = raw + dtb_ref[...]
    softplus = jnp.maximum(z, 0.0) + jnp.log(1.0 + jnp.exp(-jnp.abs(z)))
    g = -jnp.exp(alog_ref[...]) * softplus
    r = lax.broadcasted_iota(jnp.int32, (2 * CHUNK, CHUNK), 0)
    c = lax.broadcasted_iota(jnp.int32, (2 * CHUNK, CHUNK), 1)
    tri = ((r < CHUNK) & (c <= r)) | ((r >= CHUNK) & (c >= r - CHUNK))
    tri = jnp.where(tri, 1.0, 0.0).astype(BF16)
    g_hi = g.astype(BF16)
    rest = g - g_hi.astype(F32)
    g_mid = rest.astype(BF16)
    g_lo = (rest - g_mid.astype(F32)).astype(BF16)
    quarter = HEAD_DIM // 4
    lane = lax.broadcasted_iota(jnp.int32, (CHUNK, HEAD_DIM), 1)
    for n in range(tm // CHUNK):
        rows = slice(n * CHUNK, (n + 1) * CHUNK)
        sums = (jnp.dot(tri, g_hi[rows], preferred_element_type=F32)
                + jnp.dot(tri, g_mid[rows], preferred_element_type=F32)
                + jnp.dot(tri, g_lo[rows], preferred_element_type=F32))
        gate_ref[0, rows, :] = jnp.where(lane < 2 * quarter, beta[rows],
                                         jnp.where(lane < 3 * quarter, sums[:CHUNK], sums[CHUNK:]))
    for m in range(tm // HEAD_DIM):
        gate_t_ref[0, m] = gate_ref[0, m * HEAD_DIM:(m + 1) * HEAD_DIM, :].T


def _dn_prep(proj, ba, w_conv, alog_pad, dtb_pad, *, tm, key_dim, value_dim):
    bsz, s, _ = proj.shape
    qkv_dim = 2 * key_dim + value_dim
    nh = tm // SHORT_HALO
    last = s // SHORT_HALO - 1
    return pl.pallas_call(
        _dn_prep_kernel,
        out_shape=[
            jax.ShapeDtypeStruct((bsz, s, key_dim), BF16),
            jax.ShapeDtypeStruct((bsz, s, key_dim), BF16),
            jax.ShapeDtypeStruct((bsz, s, value_dim), BF16),
            jax.ShapeDtypeStruct((bsz, s, HEAD_DIM), F32),
            jax.ShapeDtypeStruct((bsz, s // HEAD_DIM, HEAD_DIM, HEAD_DIM), F32),
        ],
        grid=(bsz, s // tm),
        in_specs=[
            pl.BlockSpec((1, tm, qkv_dim), lambda b, i: (b, i, 0)),
            pl.BlockSpec((1, SHORT_HALO, qkv_dim), lambda b, i: (b, jnp.maximum(i * nh - 1, 0), 0)),
            pl.BlockSpec((1, SHORT_HALO, qkv_dim), lambda b, i: (b, jnp.minimum((i + 1) * nh, last), 0)),
            pl.BlockSpec((SHORT_CONV, qkv_dim), lambda b, i: (0, 0)),
            pl.BlockSpec((1, tm, HEAD_DIM), lambda b, i: (b, i, 0)),
            pl.BlockSpec((1, HEAD_DIM), lambda b, i: (0, 0)),
            pl.BlockSpec((1, HEAD_DIM), lambda b, i: (0, 0)),
        ],
        out_specs=[
            pl.BlockSpec((1, tm, key_dim), lambda b, i: (b, i, 0)),
            pl.BlockSpec((1, tm, key_dim), lambda b, i: (b, i, 0)),
            pl.BlockSpec((1, tm, value_dim), lambda b, i: (b, i, 0)),
            pl.BlockSpec((1, tm, HEAD_DIM), lambda b, i: (b, i, 0)),
            pl.BlockSpec((1, tm // HEAD_DIM, HEAD_DIM, HEAD_DIM), lambda b, i: (b, i, 0, 0)),
        ],
        scratch_shapes=[pltpu.VMEM((tm + 2 * SHORT_HALO, qkv_dim), F32)],
        compiler_params=_cparams(("parallel", "parallel")),
        name="dn_prep",
    )(proj, proj, proj, w_conv, ba, alog_pad, dtb_pad)


N_CHAINS = 2 * V_PER_K
WIDE = N_CHAINS * CHUNK
PAIR = V_PER_K * HEAD_DIM
OPS_U = 2 * CHUNK
OPS_AK = 3 * CHUNK
OPS_ROWS = OPS_AK + HEAD_DIM
_INTRA_GROUP = 8


def _block_diag(x):
    n = x.shape[1]
    rb = lax.broadcasted_iota(jnp.int32, (n, n), 0) // CHUNK
    cb = lax.broadcasted_iota(jnp.int32, (n, n), 1) // CHUNK
    tiled = jnp.concatenate([x.astype(BF16)] * N_CHAINS, axis=0)
    return jnp.where(rb == cb, tiled, jnp.zeros_like(tiled))


def _delta_intra_kernel(q_ref, k_ref, v_ref, g_ref, gt_ref, ops_ref, dw_ref):
    kh = pl.program_id(1)
    cs = CHUNK
    quarter = HEAD_DIM // 4
    chunks = range(q_ref.shape[1] // cs)

    r = lax.broadcasted_iota(jnp.int32, (cs, WIDE), 0)
    lw = lax.broadcasted_iota(jnp.int32, (cs, WIDE), 1)
    blk = lw // cs
    c = lw - blk * cs
    ahead = jnp.where(blk < V_PER_K, r - c, c - r)
    incl = ahead >= 0
    strict = ahead > 0
    eye = jnp.where(r == c, 1.0, 0.0)
    lane_g = lax.broadcasted_iota(jnp.int32, (cs, HEAD_DIM), 1)

    def wide(cols):
        out = cols[N_CHAINS - 1]
        for j in range(N_CHAINS - 2, -1, -1):
            out = jnp.where(blk == j, cols[j], out)
        return out

    q, k, v, grow, beta_c, g_c, decay, p, t, a_w = ([None] * len(chunks) for _ in range(10))
    g_rows = {}
    low_lanes = lax.broadcasted_iota(jnp.int32, (1, HEAD_DIM), 1) < cs

    def prepare(n):
        rows = slice(n * cs, (n + 1) * cs)
        q[n], k[n], v[n] = q_ref[0, rows, :], k_ref[0, rows, :], v_ref[0, rows, :]
        gates = g_ref[0, rows, :]
        pair, half = divmod(n, 2)
        if pair not in g_rows:
            rows4 = []
            for j in range(N_CHAINS):
                d, i = divmod(j, V_PER_K)
                rows4.append(gt_ref[0, pair, pl.ds((2 + d) * quarter + kh * V_PER_K + i, 1), :])
            g_rows[pair] = (rows4, [pltpu.roll(row, cs, axis=1) for row in rows4])
        plain, rolled = g_rows[pair]
        here, there = (plain, rolled) if half == 0 else (rolled, plain)
        grow[n] = jnp.concatenate([jnp.where(low_lanes, here[0], there[1]),
                                   jnp.where(low_lanes, here[2], there[3])], axis=1)
        beta_c[n], g_c[n] = [], []
        for j in range(N_CHAINS):
            d, i = divmod(j, V_PER_K)
            head = kh * V_PER_K + i
            beta_c[n].append(jnp.sum(jnp.where(lane_g == d * quarter + head, gates, 0.0), axis=1, keepdims=True))
            g_c[n].append(jnp.sum(jnp.where(lane_g == (2 + d) * quarter + head, gates, 0.0), axis=1,
                                  keepdims=True))
        decay[n] = jnp.where(incl, jnp.exp(jnp.where(incl, wide(g_c[n]) - grow[n], 0.0)), 0.0)
        qkk = _mm_nt(jnp.concatenate([q[n], k[n]], axis=0), jnp.concatenate([k[n]] * N_CHAINS, axis=0))
        p[n] = -jnp.where(strict, qkk[cs:] * decay[n] * wide(beta_c[n]), 0.0)
        t[n] = eye + p[n]
        a_w[n] = jnp.where(incl, qkk[:cs] * decay[n], 0.0)

    def first_level(n):
        p[n] = _mm(p[n], _block_diag(p[n]))

    def middle_level(n):
        both = _mm(jnp.concatenate([p[n], t[n]], axis=0), _block_diag(p[n]))
        p[n] = both[:cs]
        t[n] = t[n] + both[cs:]

    def last_level(n):
        t[n] = t[n] + _mm(t[n], _block_diag(p[n]))

    levels = [first_level] + [middle_level] * (cs.bit_length() - 3) + [last_level]

    def finish(n):
        kf = k[n].astype(F32)
        qf = q[n].astype(F32)
        rhs, qg, kg, dwl = [], [], [], []
        for j in range(N_CHAINS):
            d, i = divmod(j, V_PER_K)
            eg = jnp.exp(g_c[n][j])
            vj = v[n][:, i * HEAD_DIM:(i + 1) * HEAD_DIM].astype(F32)
            rhs.append(jnp.concatenate([vj * beta_c[n][j], kf * (beta_c[n][j] * eg)], axis=1).astype(BF16))
            g_last = grow[n][:, j * cs:j * cs + 1] if d == 1 else grow[n][:, (j + 1) * cs - 1:(j + 1) * cs]
            qg.append(qf * eg)
            kg.append(kf * jnp.exp(g_last - g_c[n][j]))
            dwl.append(jnp.broadcast_to(jnp.exp(g_last), (1, HEAD_DIM)))
        lhs = jnp.concatenate([jnp.where(blk == j, t[n], 0.0) for j in range(N_CHAINS)], axis=0)
        uw = _mm(lhs, jnp.concatenate(rhs, axis=0))
        for d in range(2):
            j0, j1 = d * V_PER_K, d * V_PER_K + 1
            u0, w0 = uw[j0 * cs:(j0 + 1) * cs, :HEAD_DIM], uw[j0 * cs:(j0 + 1) * cs, HEAD_DIM:]
            u1, w1 = uw[j1 * cs:(j1 + 1) * cs, :HEAD_DIM], uw[j1 * cs:(j1 + 1) * cs, HEAD_DIM:]
            kgt = jnp.concatenate([kg[j0], kg[j1]], axis=0).T
            a_pad = jnp.concatenate([a_w[n][:, d * 2 * cs:(d + 1) * 2 * cs], jnp.zeros((HEAD_DIM - cs, 2 * cs), F32)],
                                    axis=0)
            ops = jnp.concatenate([jnp.concatenate([w0, w1], axis=1),
                                   jnp.concatenate([qg[j0], qg[j1]], axis=1),
                                   jnp.concatenate([u0, u1], axis=1),
                                   jnp.concatenate([a_pad, kgt], axis=1)], axis=0)
            ops_ref[d, 0, 0, n] = ops.astype(BF16)
            dw_ref[d, 0, 0, n:n + 1, :] = jnp.concatenate([dwl[j0], dwl[j1]], axis=1)

    groups = [list(chunks)[g:g + _INTRA_GROUP] for g in range(0, len(chunks), _INTRA_GROUP)]
    for stage in range(len(groups) + 2):
        inverse = groups[stage - 1] if 1 <= stage <= len(groups) else []
        filler = [functools.partial(finish, n) for n in (groups[stage - 2] if stage >= 2 else [])]
        filler += [functools.partial(prepare, n) for n in (groups[stage] if stage < len(groups) else [])]
        per_level = -(-len(filler) // len(levels))
        for li, level in enumerate(levels):
            for n in inverse:
                level(n)
            for task in filler[li * per_level:(li + 1) * per_level]:
                task()


def _delta_intra(q, k, v, gates, gates_t, *, chunks_per_step):
    bsz, s, key_dim = q.shape
    nkh = key_dim // HEAD_DIM
    nc = s // CHUNK
    tc = chunks_per_step
    tm = tc * CHUNK
    tok = lambda w: pl.BlockSpec((1, tm, w), lambda b, h, n: (b, n, h))
    return pl.pallas_call(
        _delta_intra_kernel,
        out_shape=[jax.ShapeDtypeStruct((2, bsz, nkh, nc, OPS_ROWS, PAIR), BF16),
                   jax.ShapeDtypeStruct((2, bsz, nkh, nc, PAIR), F32)],
        grid=(bsz, nkh, nc // tc),
        in_specs=[tok(HEAD_DIM), tok(HEAD_DIM), tok(PAIR),
                  pl.BlockSpec((1, tm, HEAD_DIM), lambda b, h, n: (b, n, 0)),
                  pl.BlockSpec((1, tm // HEAD_DIM, HEAD_DIM, HEAD_DIM), lambda b, h, n: (b, n, 0, 0))],
        out_specs=[pl.BlockSpec((2, 1, 1, tc, OPS_ROWS, PAIR), lambda b, h, n: (0, b, h, n, 0, 0)),
                   pl.BlockSpec((2, 1, 1, tc, PAIR), lambda b, h, n: (0, b, h, n, 0))],
        compiler_params=_cparams(("parallel", "parallel", "parallel")),
        name="delta_intra",
    )(q, k, v, gates, gates_t)


def _delta_state_kernel(opsf_ref, dwf_ref, opsb_ref, dwb_ref, of_ref, ob_ref, state_ref):
    c = pl.program_id(2)

    @pl.when(c == 0)
    def _():
        state_ref[...] = jnp.zeros_like(state_ref)

    hb = opsf_ref.shape[2]
    zeros_s = jnp.zeros((HEAD_DIM, HEAD_DIM), BF16)
    zeros_v = jnp.zeros((CHUNK, HEAD_DIM), BF16)
    work = []
    for d, (ops_ref, dw_ref, chunk) in enumerate(((opsf_ref, dwf_ref, c), (opsb_ref, dwb_ref, pl.num_programs(2) - 1 - c))):
        for h in range(hb):
            ak = jnp.concatenate([ops_ref[0, 0, h, 0, OPS_AK:OPS_AK + CHUNK, :HEAD_DIM],
                                  ops_ref[0, 0, h, 0, OPS_AK:, HEAD_DIM:]], axis=0)
            work.append((d, h, ops_ref[0, 0, h, 0, :OPS_U, :], ops_ref[0, 0, h, 0, OPS_U:OPS_AK, :], ak,
                         dw_ref[0, 0, h, pl.ds(chunk, 1), :], state_ref[d * hb + h]))
    ws_all = []
    for d, h, wq, u, ak, dw, state in work:
        sb = state.astype(BF16)
        s_bd = jnp.concatenate([jnp.concatenate([sb[:, :HEAD_DIM], zeros_s], axis=1),
                                jnp.concatenate([zeros_s, sb[:, HEAD_DIM:]], axis=1)], axis=0)
        ws_all.append(jnp.dot(wq, s_bd, preferred_element_type=F32))
    akv_all = []
    for (d, h, wq, u, ak, dw, state), ws in zip(work, ws_all):
        v_new = (u.astype(F32) - ws[:CHUNK]).astype(BF16)
        v_bd = jnp.concatenate([jnp.concatenate([v_new[:, :HEAD_DIM], zeros_v], axis=1),
                                jnp.concatenate([zeros_v, v_new[:, HEAD_DIM:]], axis=1)], axis=0)
        akv_all.append(jnp.dot(ak, v_bd, preferred_element_type=F32))
    for (d, h, wq, u, ak, dw, state), ws, akv in zip(work, ws_all, akv_all):
        o_ref = ob_ref if d else of_ref
        o_ref[0, :, h * PAIR:(h + 1) * PAIR] = (ws[CHUNK:] + akv[:CHUNK]).astype(o_ref.dtype)
        state_ref[d * hb + h] = state * dw + akv[CHUNK:]


def _delta_state(ops, dw, *, heads_per_step):
    _, bsz, nkh, nc = ops.shape[:4]
    hb = heads_per_step
    s = nc * CHUNK
    return pl.pallas_call(
        _delta_state_kernel,
        out_shape=[jax.ShapeDtypeStruct((bsz, s, nkh * PAIR), BF16)] * 2,
        grid=(bsz, nkh // hb, nc),
        in_specs=[pl.BlockSpec((1, 1, hb, 1, OPS_ROWS, PAIR), lambda b, h, c: (0, b, h, c, 0, 0)),
                  pl.BlockSpec((1, 1, hb, nc, PAIR), lambda b, h, c: (0, b, h, 0, 0)),
                  pl.BlockSpec((1, 1, hb, 1, OPS_ROWS, PAIR), lambda b, h, c: (1, b, h, nc - 1 - c, 0, 0)),
                  pl.BlockSpec((1, 1, hb, nc, PAIR), lambda b, h, c: (1, b, h, 0, 0))],
        out_specs=[pl.BlockSpec((1, CHUNK, hb * PAIR), lambda b, h, c: (b, c, h)),
                   pl.BlockSpec((1, CHUNK, hb * PAIR), lambda b, h, c: (b, nc - 1 - c, h))],
        scratch_shapes=[pltpu.VMEM((2 * hb, HEAD_DIM, PAIR), F32)],
        compiler_params=_cparams(("parallel", "parallel", "arbitrary")),
        name="delta_state",
    )(ops, dw, ops, dw)


def _delta_rule(q, k, v, gates, gates_t):
    ops, dw = _delta_intra(q, k, v, gates, gates_t, chunks_per_step=32)
    return _delta_state(ops, dw, heads_per_step=16)


def _dn_out_kernel(of_ref, ob_ref, z_ref, nw_ref, w_ref, h_ref, o_ref, y_ref):
    for hd in range(of_ref.shape[1] // HEAD_DIM):
        cols = slice(hd * HEAD_DIM, (hd + 1) * HEAD_DIM)
        o = of_ref[:, cols].astype(F32) + ob_ref[:, cols].astype(F32)
        z = z_ref[:, cols]
        y = _rms(o, nw_ref[...]) * (z * _sigmoid(z))
        y_ref[:, cols] = y.astype(BF16)
    o_ref[...] = h_ref[...] + jnp.dot(y_ref[...], w_ref[...], preferred_element_type=F32)


def _dn_out(o_f, o_b, proj, z_block, norm_w, w_out, h, *, tm):
    m, vd = o_f.shape
    d = h.shape[1]
    return pl.pallas_call(
        _dn_out_kernel,
        out_shape=jax.ShapeDtypeStruct((m, d), F32),
        grid=(m // tm,),
        in_specs=[
            pl.BlockSpec((tm, vd), lambda i: (i, 0)),
            pl.BlockSpec((tm, vd), lambda i: (i, 0)),
            pl.BlockSpec((tm, vd), lambda i: (i, z_block)),
            pl.BlockSpec((1, HEAD_DIM), lambda i: (0, 0)),
            pl.BlockSpec((vd, d), lambda i: (0, 0), pipeline_mode=pl.Buffered(1)),
            pl.BlockSpec((tm, d), lambda i: (i, 0)),
        ],
        out_specs=pl.BlockSpec((tm, d), lambda i: (i, 0)),
        scratch_shapes=[pltpu.VMEM((tm, vd), BF16)],
        compiler_params=_cparams(("parallel",)),
        name="dn_out",
    )(o_f, o_b, proj, norm_w, w_out, h)


def _row(v):
    return v.reshape(1, -1).astype(F32)


def _conformer_layer(h, nw, w_pw1, b_pw1, w_dw, b_dw, ln_g, ln_b, w_pw2, b_pw2):
    bsz, s, d = h.shape
    u = _norm_glu(h.reshape(bsz * s, d), _row(nw), w_pw1.astype(BF16), _row(b_pw1), tm=1024, tn=512)
    return _conv_module(u.reshape(bsz, s, d), h, w_dw, _row(b_dw), _row(ln_g), _row(ln_b),
                        w_pw2.astype(BF16), _row(b_pw2), tm=256)


def _deltanet_layer(h, nw, w_in, w_conv, a_log, dt_bias, norm_w, w_out):
    bsz, s, d = h.shape
    key_dim = NUM_K_HEADS * HEAD_DIM
    value_dim = NUM_V_HEADS * HEAD_DIM
    main = 2 * key_dim + value_dim + value_dim
    h2 = h.reshape(bsz * s, d)
    proj, ba = _in_proj(h2, _row(nw), w_in.astype(BF16), main=main, tm=1024, tn=1024)
    zeros = jnp.zeros((2 * NUM_V_HEADS,), F32)
    alog_pad = jnp.concatenate([zeros, a_log.reshape(-1).astype(F32)]).reshape(1, -1)
    dtb_pad = jnp.concatenate([zeros, dt_bias.reshape(-1).astype(F32)]).reshape(1, -1)
    q, k, v, gates, gates_t = _dn_prep(proj.reshape(bsz, s, main), ba.reshape(bsz, s, -1), w_conv, alog_pad,
                                       dtb_pad, tm=256, key_dim=key_dim, value_dim=value_dim)
    o_f, o_b = _delta_rule(q, k, v, gates, gates_t)
    out = _dn_out(o_f.reshape(bsz * s, value_dim), o_b.reshape(bsz * s, value_dim), proj,
                  (2 * key_dim + value_dim) // value_dim, _row(norm_w), w_out.astype(BF16), h2, tm=256)
    return out.reshape(bsz, s, d)


def kernel(x, mix_norm, ffn_norm, final_norm, cv_w_pw1, cv_b_pw1, cv_w_dw, cv_b_dw, cv_ln_g, cv_ln_b,
           cv_w_pw2, cv_b_pw2, dn_w_in, dn_w_conv, dn_a_log, dn_dt_bias, dn_norm_w, dn_w_out,
           ffn_w_gate_up, ffn_w_down):
    bsz, s, d = x.shape
    depth = mix_norm.shape[0]
    w_gate_up = ffn_w_gate_up.astype(BF16)
    w_down = ffn_w_down.astype(BF16)
    h = x
    for i in range(depth):
        j = i // 2
        if i % 2 == 0:
            h = _conformer_layer(h, mix_norm[i], cv_w_pw1[j], cv_b_pw1[j], cv_w_dw[j], cv_b_dw[j],
                                 cv_ln_g[j], cv_ln_b[j], cv_w_pw2[j], cv_b_pw2[j])
        else:
            h = _deltanet_layer(h, mix_norm[i], dn_w_in[j], dn_w_conv[j], dn_a_log[j], dn_dt_bias[j],
                                dn_norm_w[j], dn_w_out[j])
        last = i == depth - 1
        h = _ffn(h.reshape(bsz * s, d), _row(ffn_norm[i]), w_gate_up, w_down, _row(final_norm), layer=i,
                 tm=512, th=512, final_norm=last).reshape(bsz, s, d)
    return h
```

```python
import functools

import jax
import jax.numpy as jnp
from jax import lax
from jax.experimental import pallas as pl
from jax.experimental.pallas import tpu as pltpu

F32 = jnp.float32
BF16 = jnp.bfloat16

RMS_EPS = 1e-6
LN_EPS = 1e-5
L2_EPS = 1e-6

CONV_KERNEL = 31
CONV_HALO = 16
SHORT_CONV = 5
SHORT_HALO = 8

SUBLANES = 8
HEAD_DIM = 128
NUM_K_HEADS = 16
NUM_V_HEADS = 32
V_PER_K = NUM_V_HEADS // NUM_K_HEADS
CHUNK = 64

VMEM_LIMIT = 56 * 1024 * 1024


def _cparams(sem):
    return pltpu.CompilerParams(dimension_semantics=sem, vmem_limit_bytes=VMEM_LIMIT)


def _mm(a, b):
    return jnp.dot(a.astype(BF16), b.astype(BF16), preferred_element_type=F32)


def _mm_nt(a, b):
    return lax.dot_general(a.astype(BF16), b.astype(BF16), (((1,), (1,)), ((), ())),
                           preferred_element_type=F32)


def _rms(x, w):
    ms = jnp.mean(x * x, axis=-1, keepdims=True)
    return x * lax.rsqrt(ms + RMS_EPS) * w


def _sigmoid(x):
    return 1.0 / (1.0 + jnp.exp(-x))


def _in_proj_kernel(h_ref, nw_ref, w_ref, wt_ref, o_ref, tail_ref, hn_ref):
    @pl.when(pl.program_id(1) == 0)
    def _():
        hn = _rms(h_ref[...], nw_ref[...]).astype(BF16)
        hn_ref[...] = hn
        tail_ref[...] = jnp.dot(hn, wt_ref[...], preferred_element_type=F32)

    o_ref[...] = jnp.dot(hn_ref[...], w_ref[...], preferred_element_type=F32)


def _in_proj(h, nw, w, *, main, tm, tn):
    m, d = h.shape
    tail = w.shape[1] - main
    return pl.pallas_call(
        _in_proj_kernel,
        out_shape=[jax.ShapeDtypeStruct((m, main), F32), jax.ShapeDtypeStruct((m, tail), F32)],
        grid=(m // tm, main // tn),
        in_specs=[
            pl.BlockSpec((tm, d), lambda i, j: (i, 0)),
            pl.BlockSpec((1, d), lambda i, j: (0, 0)),
            pl.BlockSpec((d, tn), lambda i, j: (0, j)),
            pl.BlockSpec((d, tail), lambda i, j: (0, main // tail)),
        ],
        out_specs=[pl.BlockSpec((tm, tn), lambda i, j: (i, j)),
                   pl.BlockSpec((tm, tail), lambda i, j: (i, 0))],
        scratch_shapes=[pltpu.VMEM((tm, d), BF16)],
        compiler_params=_cparams(("parallel", "arbitrary")),
        name="in_proj",
    )(h, nw, w, w)


def _norm_glu_kernel(h_ref, nw_ref, wa_ref, wg_ref, ba_ref, bg_ref, o_ref, hn_ref):
    @pl.when(pl.program_id(1) == 0)
    def _():
        hn_ref[...] = _rms(h_ref[...], nw_ref[...]).astype(BF16)

    hn = hn_ref[...]
    a = jnp.dot(hn, wa_ref[...], preferred_element_type=F32) + ba_ref[...]
    g = jnp.dot(hn, wg_ref[...], preferred_element_type=F32) + bg_ref[...]
    o_ref[...] = a * _sigmoid(g)


def _norm_glu(h, nw, w, b, *, tm, tn):
    m, d = h.shape
    n = w.shape[1] // 2
    nb = n // tn
    return pl.pallas_call(
        _norm_glu_kernel,
        out_shape=jax.ShapeDtypeStruct((m, n), F32),
        grid=(m // tm, nb),
        in_specs=[
            pl.BlockSpec((tm, d), lambda i, j: (i, 0)),
            pl.BlockSpec((1, d), lambda i, j: (0, 0)),
            pl.BlockSpec((d, tn), lambda i, j: (0, j)),
            pl.BlockSpec((d, tn), lambda i, j: (0, j + nb)),
            pl.BlockSpec((1, tn), lambda i, j: (0, j)),
            pl.BlockSpec((1, tn), lambda i, j: (0, j + nb)),
        ],
        out_specs=pl.BlockSpec((tm, tn), lambda i, j: (i, j)),
        scratch_shapes=[pltpu.VMEM((tm, d), BF16)],
        compiler_params=_cparams(("parallel", "arbitrary")),
        name="norm_glu",
    )(h, nw, w, w, b, b)


_CONV_ROWS = 64
_CONV_LANES = 256


def _conv_module_kernel(uc_ref, up_ref, un_ref, h_ref, wdw_ref, bdw_ref, g_ref, b_ref, w2_ref, b2_ref,
                        o_ref, ext_ref, cv_ref, sh_ref):
    i = pl.program_id(1)
    tm = uc_ref.shape[1]
    d = uc_ref.shape[2]
    pad = (CONV_KERNEL - 1) // 2
    ext_ref[0:CONV_HALO, :] = jnp.where(i > 0, up_ref[0], 0.0)
    ext_ref[CONV_HALO:CONV_HALO + tm, :] = uc_ref[0]
    ext_ref[CONV_HALO + tm:, :] = jnp.where(i < pl.num_programs(1) - 1, un_ref[0], 0.0)

    base = CONV_HALO - pad
    span = sh_ref.shape[1]

    def lane_block(cb, carry):
        cols = pl.ds(pl.multiple_of(cb * _CONV_LANES, _CONV_LANES), _CONV_LANES)
        for s in range(1, SUBLANES):
            sh_ref[s - 1] = ext_ref[s:s + span, cols]
        for r0 in range(0, tm, _CONV_ROWS):
            acc = jnp.zeros((_CONV_ROWS, _CONV_LANES), F32)
            for j in range(CONV_KERNEL):
                s = (base + j) % SUBLANES
                a = r0 + base + j - s
                rows = ext_ref[a:a + _CONV_ROWS, cols] if s == 0 else sh_ref[s - 1, a:a + _CONV_ROWS, :]
                acc = acc + wdw_ref[j:j + 1, cols] * rows
            cv_ref[r0:r0 + _CONV_ROWS, cols] = acc + bdw_ref[:, cols]
        return carry

    lax.fori_loop(0, d // _CONV_LANES, lane_block, 0)

    x = cv_ref[...]
    mu = jnp.mean(x, axis=-1, keepdims=True)
    xc = x - mu
    var = jnp.mean(xc * xc, axis=-1, keepdims=True)
    y = xc * lax.rsqrt(var + LN_EPS) * g_ref[...] + b_ref[...]
    y = y * _sigmoid(y)
    o_ref[0] = h_ref[0] + jnp.dot(y.astype(BF16), w2_ref[...], preferred_element_type=F32) + b2_ref[...]


def _conv_module(u, h, wdw, bdw, ln_g, ln_b, w2, b2, *, tm):
    bsz, s, d = u.shape
    nh = tm // CONV_HALO
    last = s // CONV_HALO - 1
    tile = pl.BlockSpec((1, tm, d), lambda b, i: (b, i, 0))
    vec = pl.BlockSpec((1, d), lambda b, i: (0, 0))
    return pl.pallas_call(
        _conv_module_kernel,
        out_shape=jax.ShapeDtypeStruct((bsz, s, d), F32),
        grid=(bsz, s // tm),
        in_specs=[
            tile,
            pl.BlockSpec((1, CONV_HALO, d), lambda b, i: (b, jnp.maximum(i * nh - 1, 0), 0)),
            pl.BlockSpec((1, CONV_HALO, d), lambda b, i: (b, jnp.minimum((i + 1) * nh, last), 0)),
            tile,
            pl.BlockSpec((CONV_KERNEL, d), lambda b, i: (0, 0)),
            vec, vec, vec,
            pl.BlockSpec((d, d), lambda b, i: (0, 0), pipeline_mode=pl.Buffered(1)),
            vec,
        ],
        out_specs=tile,
        scratch_shapes=[pltpu.VMEM((tm + 2 * CONV_HALO, d), F32), pltpu.VMEM((tm, d), F32),
                        pltpu.VMEM((SUBLANES - 1, tm + 2 * CONV_HALO - SUBLANES, _CONV_LANES), F32)],
        compiler_params=_cparams(("parallel", "parallel")),
        name="conv_module",
    )(u, u, u, h, wdw, bdw, ln_g, ln_b, w2, b2)


def _ffn_kernel(h_ref, nw_ref, wg_ref, wu_ref, wd_ref, fw_ref, o_ref, hn_ref, acc_ref, *, final_norm):
    j = pl.program_id(1)

    @pl.when(j == 0)
    def _():
        hn_ref[...] = _rms(h_ref[...], nw_ref[...]).astype(BF16)
        acc_ref[...] = jnp.zeros_like(acc_ref)

    hn = hn_ref[...]
    g = jnp.dot(hn, wg_ref[...], preferred_element_type=F32)
    u = jnp.dot(hn, wu_ref[...], preferred_element_type=F32)
    a = (g * _sigmoid(g) * u).astype(BF16)
    acc_ref[...] += jnp.dot(a, wd_ref[...], preferred_element_type=F32)

    @pl.when(j == pl.num_programs(1) - 1)
    def _():
        y = h_ref[...] + acc_ref[...]
        if final_norm:
            y = _rms(y, fw_ref[...])
        o_ref[...] = y


def _ffn(h, nw, w_gate_up, w_down, fw, *, layer, tm, th, final_norm):
    m, d = h.shape
    hid = w_down.shape[1]
    nb = hid // th
    vec = pl.BlockSpec((1, d), lambda i, j: (0, 0))
    return pl.pallas_call(
        functools.partial(_ffn_kernel, final_norm=final_norm),
        out_shape=jax.ShapeDtypeStruct((m, d), F32),
        grid=(m // tm, nb),
        in_specs=[
            pl.BlockSpec((tm, d), lambda i, j: (i, 0)),
            vec,
            pl.BlockSpec((None, d, th), lambda i, j: (layer, 0, j)),
            pl.BlockSpec((None, d, th), lambda i, j: (layer, 0, j + nb)),
            pl.BlockSpec((None, th, d), lambda i, j: (layer, j, 0)),
            vec,
        ],
        out_specs=pl.BlockSpec((tm, d), lambda i, j: (i, 0)),
        scratch_shapes=[pltpu.VMEM((tm, d), BF16), pltpu.VMEM((tm, d), F32)],
        compiler_params=_cparams(("parallel", "arbitrary")),
        name="ffn",
    )(h, nw, w_gate_up, w_gate_up, w_down, fw)


def _dn_prep_kernel(xc_ref, xp_ref, xn_ref, wc_ref, ba_ref, alog_ref, dtb_ref,
                    q_ref, k_ref, v_ref, gate_ref, gate_t_ref, ext_ref):
    i = pl.program_id(1)
    tm = xc_ref.shape[1]
    key_dim = q_ref.shape[2]
    pad = (SHORT_CONV - 1) // 2
    ext_ref[0:SHORT_HALO, :] = jnp.where(i > 0, xp_ref[0], 0.0)
    ext_ref[SHORT_HALO:SHORT_HALO + tm, :] = xc_ref[0]
    ext_ref[SHORT_HALO + tm:, :] = jnp.where(i < pl.num_programs(1) - 1, xn_ref[0], 0.0)

    base = SHORT_HALO - pad
    n_heads = xc_ref.shape[2] // HEAD_DIM
    for hd in range(n_heads):
        cols = slice(hd * HEAD_DIM, (hd + 1) * HEAD_DIM)
        taps = sorted(range(SHORT_CONV), key=lambda j: (base + j) % SUBLANES)
        acc = wc_ref[taps[0]:taps[0] + 1, cols] * ext_ref[base + taps[0]:base + taps[0] + tm, cols]
        for j in taps[1:]:
            acc = acc + wc_ref[j:j + 1, cols] * ext_ref[base + j:base + j + tm, cols]
        y = acc * _sigmoid(acc)
        c0 = hd * HEAD_DIM
        if c0 < 2 * key_dim:
            inv = lax.rsqrt(jnp.sum(y * y, axis=-1, keepdims=True) + L2_EPS)
            if c0 < key_dim:
                q_ref[0, :, cols] = (y * (inv * (HEAD_DIM ** -0.5))).astype(BF16)
            else:
                k_ref[0, :, c0 - key_dim:c0 - key_dim + HEAD_DIM] = (y * inv).astype(BF16)
        else:
            v_ref[0, :, c0 - 2 * key_dim:c0 - 2 * key_dim + HEAD_DIM] = y.astype(BF16)

    raw = ba_ref[0]
    beta = _sigmoid(raw)
    z = raw + dtb_ref[...]
    softplus = jnp.maximum(z, 0.0) + jnp.log(1.0 + jnp.exp(-jnp.abs(z)))
    g = -jnp.exp(alog_ref[...]) * softplus
    r = lax.broadcasted_iota(jnp.int32, (2 * CHUNK, CHUNK), 0)
    c = lax.broadcasted_iota(jnp.int32, (2 * CHUNK, CHUNK), 1)
    tri = ((r < CHUNK) & (c <= r)) | ((r >= CHUNK) & (c >= r - CHUNK))
    tri = jnp.where(tri, 1.0, 0.0).astype(BF16)
    g_hi = g.astype(BF16)
    rest = g - g_hi.astype(F32)
    g_mid = rest.astype(BF16)
    g_lo = (rest - g_mid.astype(F32)).astype(BF16)
    quarter = HEAD_DIM // 4
    lane = lax.broadcasted_iota(jnp.int32, (CHUNK, HEAD_DIM), 1)
    for n in range(tm // CHUNK):
        rows = slice(n * CHUNK, (n + 1) * CHUNK)
        sums = (jnp.dot(tri, g_hi[rows], preferred_element_type=F32)
                + jnp.dot(tri, g_mid[rows], preferred_element_type=F32)
                + jnp.dot(tri, g_lo[rows], preferred_element_type=F32))
        gate_ref[0, rows, :] = jnp.where(lane < 2 * quarter, beta[rows],
                                         jnp.where(lane < 3 * quarter, sums[:CHUNK], sums[CHUNK:]))
    for m in range(tm // HEAD_DIM):
        gate_t_ref[0, m] = gate_ref[0, m * HEAD_DIM:(m + 1) * HEAD_DIM, :].T


def _dn_prep(proj, ba, w_conv, alog_pad, dtb_pad, *, tm, key_dim, value_dim):
    bsz, s, _ = proj.shape
    qkv_dim = 2 * key_dim + value_dim
    nh = tm // SHORT_HALO
    last = s // SHORT_HALO - 1
    return pl.pallas_call(
        _dn_prep_kernel,
        out_shape=[
            jax.ShapeDtypeStruct((bsz, s, key_dim), BF16),
            jax.ShapeDtypeStruct((bsz, s, key_dim), BF16),
            jax.ShapeDtypeStruct((bsz, s, value_dim), BF16),
            jax.ShapeDtypeStruct((bsz, s, HEAD_DIM), F32),
            jax.ShapeDtypeStruct((bsz, s // HEAD_DIM, HEAD_DIM, HEAD_DIM), F32),
        ],
        grid=(bsz, s // tm),
        in_specs=[
            pl.BlockSpec((1, tm, qkv_dim), lambda b, i: (b, i, 0)),
            pl.BlockSpec((1, SHORT_HALO, qkv_dim), lambda b, i: (b, jnp.maximum(i * nh - 1, 0), 0)),
            pl.BlockSpec((1, SHORT_HALO, qkv_dim), lambda b, i: (b, jnp.minimum((i + 1) * nh, last), 0)),
            pl.BlockSpec((SHORT_CONV, qkv_dim), lambda b, i: (0, 0)),
            pl.BlockSpec((1, tm, HEAD_DIM), lambda b, i: (b, i, 0)),
            pl.BlockSpec((1, HEAD_DIM), lambda b, i: (0, 0)),
            pl.BlockSpec((1, HEAD_DIM), lambda b, i: (0, 0)),
        ],
        out_specs=[
            pl.BlockSpec((1, tm, key_dim), lambda b, i: (b, i, 0)),
            pl.BlockSpec((1, tm, key_dim), lambda b, i: (b, i, 0)),
            pl.BlockSpec((1, tm, value_dim), lambda b, i: (b, i, 0)),
            pl.BlockSpec((1, tm, HEAD_DIM), lambda b, i: (b, i, 0)),
            pl.BlockSpec((1, tm // HEAD_DIM, HEAD_DIM, HEAD_DIM), lambda b, i: (b, i, 0, 0)),
        ],
        scratch_shapes=[pltpu.VMEM((tm + 2 * SHORT_HALO, qkv_dim), F32)],
        compiler_params=_cparams(("parallel", "parallel")),
        name="dn_prep",
    )(proj, proj, proj, w_conv, ba, alog_pad, dtb_pad)


N_CHAINS = 2 * V_PER_K
WIDE = N_CHAINS * CHUNK
PAIR = V_PER_K * HEAD_DIM
OPS_U = 2 * CHUNK
OPS_AK = 3 * CHUNK
OPS_ROWS = OPS_AK + HEAD_DIM
_INTRA_GROUP = 8


def _block_diag(x):
    n = x.shape[1]
    rb = lax.broadcasted_iota(jnp.int32, (n, n), 0) // CHUNK
    cb = lax.broadcasted_iota(jnp.int32, (n, n), 1) // CHUNK
    tiled = jnp.concatenate([x.astype(BF16)] * N_CHAINS, axis=0)
    return jnp.where(rb == cb, tiled, jnp.zeros_like(tiled))


def _delta_intra_kernel(q_ref, k_ref, v_ref, g_ref, gt_ref, ops_ref, dw_ref):
    kh = pl.program_id(1)
    cs = CHUNK
    quarter = HEAD_DIM // 4
    chunks = range(q_ref.shape[1] // cs)

    r = lax.broadcasted_iota(jnp.int32, (cs, WIDE), 0)
    lw = lax.broadcasted_iota(jnp.int32, (cs, WIDE), 1)
    blk = lw // cs
    c = lw - blk * cs
    ahead = jnp.where(blk < V_PER_K, r - c, c - r)
    incl = ahead >= 0
    strict = ahead > 0
    eye = jnp.where(r == c, 1.0, 0.0)
    lane_g = lax.broadcasted_iota(jnp.int32, (cs, HEAD_DIM), 1)

    def wide(cols):
        out = cols[N_CHAINS - 1]
        for j in range(N_CHAINS - 2, -1, -1):
            out = jnp.where(blk == j, cols[j], out)
        return out

    q, k, v, grow, beta_c, g_c, decay, p, t, a_w = ([None] * len(chunks) for _ in range(10))
    g_rows = {}
    low_lanes = lax.broadcasted_iota(jnp.int32, (1, HEAD_DIM), 1) < cs

    def prepare(n):
        rows = slice(n * cs, (n + 1) * cs)
        q[n], k[n], v[n] = q_ref[0, rows, :], k_ref[0, rows, :], v_ref[0, rows, :]
        gates = g_ref[0, rows, :]
        pair, half = divmod(n, 2)
        if pair not in g_rows:
            rows4 = []
            for j in range(N_CHAINS):
                d, i = divmod(j, V_PER_K)
                rows4.append(gt_ref[0, pair, pl.ds((2 + d) * quarter + kh * V_PER_K + i, 1), :])
            g_rows[pair] = (rows4, [pltpu.roll(row, cs, axis=1) for row in rows4])
        plain, rolled = g_rows[pair]
        here, there = (plain, rolled) if half == 0 else (rolled, plain)
        grow[n] = jnp.concatenate([jnp.where(low_lanes, here[0], there[1]),
                                   jnp.where(low_lanes, here[2], there[3])], axis=1)
        beta_c[n], g_c[n] = [], []
        for j in range(N_CHAINS):
            d, i = divmod(j, V_PER_K)
            head = kh * V_PER_K + i
            beta_c[n].append(jnp.sum(jnp.where(lane_g == d * quarter + head, gates, 0.0), axis=1, keepdims=True))
            g_c[n].append(jnp.sum(jnp.where(lane_g == (2 + d) * quarter + head, gates, 0.0), axis=1,
                                  keepdims=True))
        decay[n] = jnp.where(incl, jnp.exp(jnp.where(incl, wide(g_c[n]) - grow[n], 0.0)), 0.0)
        qkk = _mm_nt(jnp.concatenate([q[n], k[n]], axis=0), jnp.concatenate([k[n]] * N_CHAINS, axis=0))
        p[n] = -jnp.where(strict, qkk[cs:] * decay[n] * wide(beta_c[n]), 0.0)
        t[n] = eye + p[n]
        a_w[n] = jnp.where(incl, qkk[:cs] * decay[n], 0.0)

    def first_level(n):
        p[n] = _mm(p[n], _block_diag(p[n]))

    def middle_level(n):
        both = _mm(jnp.concatenate([p[n], t[n]], axis=0), _block_diag(p[n]))
        p[n] = both[:cs]
        t[n] = t[n] + both[cs:]

    def last_level(n):
        t[n] = t[n] + _mm(t[n], _block_diag(p[n]))

    levels = [first_level] + [middle_level] * (cs.bit_length() - 3) + [last_level]

    def finish(n):
        kf = k[n].astype(F32)
        qf = q[n].astype(F32)
        rhs, qg, kg, dwl = [], [], [], []
        for j in range(N_CHAINS):
            d, i = divmod(j, V_PER_K)
            eg = jnp.exp(g_c[n][j])
            vj = v[n][:, i * HEAD_DIM:(i + 1) * HEAD_DIM].astype(F32)
            rhs.append(jnp.concatenate([vj * beta_c[n][j], kf * (beta_c[n][j] * eg)], axis=1).astype(BF16))
            g_last = grow[n][:, j * cs:j * cs + 1] if d == 1 else grow[n][:, (j + 1) * cs - 1:(j + 1) * cs]
            qg.append(qf * eg)
            kg.append(kf * jnp.exp(g_last - g_c[n][j]))
            dwl.append(jnp.broadcast_to(jnp.exp(g_last), (1, HEAD_DIM)))
        lhs = jnp.concatenate([jnp.where(blk == j, t[n], 0.0) for j in range(N_CHAINS)], axis=0)
        uw = _mm(lhs, jnp.concatenate(rhs, axis=0))
        for d in range(2):
            j0, j1 = d * V_PER_K, d * V_PER_K + 1
            u0, w0 = uw[j0 * cs:(j0 + 1) * cs, :HEAD_DIM], uw[j0 * cs:(j0 + 1) * cs, HEAD_DIM:]
            u1, w1 = uw[j1 * cs:(j1 + 1) * cs, :HEAD_DIM], uw[j1 * cs:(j1 + 1) * cs, HEAD_DIM:]
            kgt = jnp.concatenate([kg[j0], kg[j1]], axis=0).T
            a_pad = jnp.concatenate([a_w[n][:, d * 2 * cs:(d + 1) * 2 * cs], jnp.zeros((HEAD_DIM - cs, 2 * cs), F32)],
                                    axis=0)
            ops = jnp.concatenate([jnp.concatenate([w0, w1], axis=1),
                                   jnp.concatenate([qg[j0], qg[j1]], axis=1),
                                   jnp.concatenate([u0, u1], axis=1),
                                   jnp.concatenate([a_pad, kgt], axis=1)], axis=0)
            ops_ref[d, 0, 0, n] = ops.astype(BF16)
            dw_ref[d, 0, 0, n:n + 1, :] = jnp.concatenate([dwl[j0], dwl[j1]], axis=1)

    groups = [list(chunks)[g:g + _INTRA_GROUP] for g in range(0, len(chunks), _INTRA_GROUP)]
    for stage in range(len(groups) + 2):
        inverse = groups[stage - 1] if 1 <= stage <= len(groups) else []
        filler = [functools.partial(finish, n) for n in (groups[stage - 2] if stage >= 2 else [])]
        filler += [functools.partial(prepare, n) for n in (groups[stage] if stage < len(groups) else [])]
        per_level = -(-len(filler) // len(levels))
        for li, level in enumerate(levels):
            for n in inverse:
                level(n)
            for task in filler[li * per_level:(li + 1) * per_level]:
                task()


def _delta_intra(q, k, v, gates, gates_t, *, chunks_per_step):
    bsz, s, key_dim = q.shape
    nkh = key_dim // HEAD_DIM
    nc = s // CHUNK
    tc = chunks_per_step
    tm = tc * CHUNK
    tok = lambda w: pl.BlockSpec((1, tm, w), lambda b, h, n: (b, n, h))
    return pl.pallas_call(
        _delta_intra_kernel,
        out_shape=[jax.ShapeDtypeStruct((2, bsz, nkh, nc, OPS_ROWS, PAIR), BF16),
                   jax.ShapeDtypeStruct((2, bsz, nkh, nc, PAIR), F32)],
        grid=(bsz, nkh, nc // tc),
        in_specs=[tok(HEAD_DIM), tok(HEAD_DIM), tok(PAIR),
                  pl.BlockSpec((1, tm, HEAD_DIM), lambda b, h, n: (b, n, 0)),
                  pl.BlockSpec((1, tm // HEAD_DIM, HEAD_DIM, HEAD_DIM), lambda b, h, n: (b, n, 0, 0))],
        out_specs=[pl.BlockSpec((2, 1, 1, tc, OPS_ROWS, PAIR), lambda b, h, n: (0, b, h, n, 0, 0)),
                   pl.BlockSpec((2, 1, 1, tc, PAIR), lambda b, h, n: (0, b, h, n, 0))],
        compiler_params=_cparams(("parallel", "parallel", "parallel")),
        name="delta_intra",
    )(q, k, v, gates, gates_t)


_STATE_RING = 3


def _delta_state_kernel(ops_hbm, dwf_ref, dwb_ref, of_ref, ob_ref, state_ref, ring_ref, sem_ref):
    b = pl.program_id(0)
    c = pl.program_id(1)
    nc = pl.num_programs(1)
    hb = ring_ref.shape[2]
    t = b * nc + c
    total = pl.num_programs(0) * nc

    def copies(step):
        sb = step // nc
        sc = step - sb * nc
        slot = step % _STATE_RING
        return (pltpu.make_async_copy(ops_hbm.at[0, sb, pl.ds(0, hb), sc], ring_ref.at[slot, 0], sem_ref.at[slot, 0]),
                pltpu.make_async_copy(ops_hbm.at[1, sb, pl.ds(0, hb), nc - 1 - sc], ring_ref.at[slot, 1],
                                      sem_ref.at[slot, 1]))

    @pl.when(t == 0)
    def _():
        for step in range(_STATE_RING - 1):
            for cp in copies(step):
                cp.start()

    @pl.when(t + (_STATE_RING - 1) < total)
    def _():
        for cp in copies(t + (_STATE_RING - 1)):
            cp.start()

    for cp in copies(t):
        cp.wait()

    @pl.when(c == 0)
    def _():
        state_ref[...] = jnp.zeros_like(state_ref)

    slot = t % _STATE_RING
    zeros_s = jnp.zeros((HEAD_DIM, HEAD_DIM), BF16)
    zeros_v = jnp.zeros((CHUNK, HEAD_DIM), BF16)
    work = []
    for d, (dw_ref, chunk) in enumerate(((dwf_ref, c), (dwb_ref, nc - 1 - c))):
        for h in range(hb):
            ak = jnp.concatenate([ring_ref[slot, d, h, OPS_AK:OPS_AK + CHUNK, :HEAD_DIM],
                                  ring_ref[slot, d, h, OPS_AK:, HEAD_DIM:]], axis=0)
            work.append((d, h, ring_ref[slot, d, h, :OPS_U, :], ring_ref[slot, d, h, OPS_U:OPS_AK, :], ak,
                         dw_ref[0, 0, h, pl.ds(chunk, 1), :], state_ref[d * hb + h]))
    ws_all = []
    for d, h, wq, u, ak, dw, state in work:
        sb = state.astype(BF16)
        s_bd = jnp.concatenate([jnp.concatenate([sb[:, :HEAD_DIM], zeros_s], axis=1),
                                jnp.concatenate([zeros_s, sb[:, HEAD_DIM:]], axis=1)], axis=0)
        ws_all.append(jnp.dot(wq, s_bd, preferred_element_type=F32))
    akv_all = []
    for (d, h, wq, u, ak, dw, state), ws in zip(work, ws_all):
        v_new = (u.astype(F32) - ws[:CHUNK]).astype(BF16)
        v_bd = jnp.concatenate([jnp.concatenate([v_new[:, :HEAD_DIM], zeros_v], axis=1),
                                jnp.concatenate([zeros_v, v_new[:, HEAD_DIM:]], axis=1)], axis=0)
        akv_all.append(jnp.dot(ak, v_bd, preferred_element_type=F32))
    for (d, h, wq, u, ak, dw, state), ws, akv in zip(work, ws_all, akv_all):
        o_ref = ob_ref if d else of_ref
        o_ref[0, :, h * PAIR:(h + 1) * PAIR] = (ws[CHUNK:] + akv[:CHUNK]).astype(o_ref.dtype)
        state_ref[d * hb + h] = state * dw + akv[CHUNK:]


def _delta_state(ops, dw):
    _, bsz, nkh, nc = ops.shape[:4]
    s = nc * CHUNK
    return pl.pallas_call(
        _delta_state_kernel,
        out_shape=[jax.ShapeDtypeStruct((bsz, s, nkh * PAIR), BF16)] * 2,
        grid=(bsz, nc),
        in_specs=[pl.BlockSpec(memory_space=pl.ANY),
                  pl.BlockSpec((1, 1, nkh, nc, PAIR), lambda b, c: (0, b, 0, 0, 0)),
                  pl.BlockSpec((1, 1, nkh, nc, PAIR), lambda b, c: (1, b, 0, 0, 0))],
        out_specs=[pl.BlockSpec((1, CHUNK, nkh * PAIR), lambda b, c: (b, c, 0)),
                   pl.BlockSpec((1, CHUNK, nkh * PAIR), lambda b, c: (b, nc - 1 - c, 0))],
        scratch_shapes=[pltpu.VMEM((2 * nkh, HEAD_DIM, PAIR), F32),
                        pltpu.VMEM((_STATE_RING, 2, nkh, OPS_ROWS, PAIR), BF16),
                        pltpu.SemaphoreType.DMA((_STATE_RING, 2))],
        compiler_params=_cparams(("arbitrary", "arbitrary")),
        name="delta_state",
    )(ops, dw, dw)


def _delta_rule(q, k, v, gates, gates_t):
    ops, dw = _delta_intra(q, k, v, gates, gates_t, chunks_per_step=32)
    return _delta_state(ops, dw)


def _dn_out_kernel(of_ref, ob_ref, z_ref, nw_ref, w_ref, h_ref, o_ref, y_ref):
    for hd in range(of_ref.shape[1] // HEAD_DIM):
        cols = slice(hd * HEAD_DIM, (hd + 1) * HEAD_DIM)
        o = of_ref[:, cols].astype(F32) + ob_ref[:, cols].astype(F32)
        z = z_ref[:, cols]
        y = _rms(o, nw_ref[...]) * (z * _sigmoid(z))
        y_ref[:, cols] = y.astype(BF16)
    o_ref[...] = h_ref[...] + jnp.dot(y_ref[...], w_ref[...], preferred_element_type=F32)


def _dn_out(o_f, o_b, proj, z_block, norm_w, w_out, h, *, tm):
    m, vd = o_f.shape
    d = h.shape[1]
    return pl.pallas_call(
        _dn_out_kernel,
        out_shape=jax.ShapeDtypeStruct((m, d), F32),
        grid=(m // tm,),
        in_specs=[
            pl.BlockSpec((tm, vd), lambda i: (i, 0)),
            pl.BlockSpec((tm, vd), lambda i: (i, 0)),
            pl.BlockSpec((tm, vd), lambda i: (i, z_block)),
            pl.BlockSpec((1, HEAD_DIM), lambda i: (0, 0)),
            pl.BlockSpec((vd, d), lambda i: (0, 0), pipeline_mode=pl.Buffered(1)),
            pl.BlockSpec((tm, d), lambda i: (i, 0)),
        ],
        out_specs=pl.BlockSpec((tm, d), lambda i: (i, 0)),
        scratch_shapes=[pltpu.VMEM((tm, vd), BF16)],
        compiler_params=_cparams(("parallel",)),
        name="dn_out",
    )(o_f, o_b, proj, norm_w, w_out, h)


def _row(v):
    return v.reshape(1, -1).astype(F32)


def _conformer_layer(h, nw, w_pw1, b_pw1, w_dw, b_dw, ln_g, ln_b, w_pw2, b_pw2):
    bsz, s, d = h.shape
    u = _norm_glu(h.reshape(bsz * s, d), _row(nw), w_pw1.astype(BF16), _row(b_pw1), tm=1024, tn=512)
    return _conv_module(u.reshape(bsz, s, d), h, w_dw, _row(b_dw), _row(ln_g), _row(ln_b),
                        w_pw2.astype(BF16), _row(b_pw2), tm=256)


def _deltanet_layer(h, nw, w_in, w_conv, a_log, dt_bias, norm_w, w_out):
    bsz, s, d = h.shape
    key_dim = NUM_K_HEADS * HEAD_DIM
    value_dim = NUM_V_HEADS * HEAD_DIM
    main = 2 * key_dim + value_dim + value_dim
    h2 = h.reshape(bsz * s, d)
    proj, ba = _in_proj(h2, _row(nw), w_in.astype(BF16), main=main, tm=1024, tn=1024)
    zeros = jnp.zeros((2 * NUM_V_HEADS,), F32)
    alog_pad = jnp.concatenate([zeros, a_log.reshape(-1).astype(F32)]).reshape(1, -1)
    dtb_pad = jnp.concatenate([zeros, dt_bias.reshape(-1).astype(F32)]).reshape(1, -1)
    q, k, v, gates, gates_t = _dn_prep(proj.reshape(bsz, s, main), ba.reshape(bsz, s, -1), w_conv, alog_pad,
                                       dtb_pad, tm=256, key_dim=key_dim, value_dim=value_dim)
    o_f, o_b = _delta_rule(q, k, v, gates, gates_t)
    out = _dn_out(o_f.reshape(bsz * s, value_dim), o_b.reshape(bsz * s, value_dim), proj,
                  (2 * key_dim + value_dim) // value_dim, _row(norm_w), w_out.astype(BF16), h2, tm=256)
    return out.reshape(bsz, s, d)


def kernel(x, mix_norm, ffn_norm, final_norm, cv_w_pw1, cv_b_pw1, cv_w_dw, cv_b_dw, cv_ln_g, cv_ln_b,
           cv_w_pw2, cv_b_pw2, dn_w_in, dn_w_conv, dn_a_log, dn_dt_bias, dn_norm_w, dn_w_out,
           ffn_w_gate_up, ffn_w_down):
    bsz, s, d = x.shape
    depth = mix_norm.shape[0]
    w_gate_up = ffn_w_gate_up.astype(BF16)
    w_down = ffn_w_down.astype(BF16)
    h = x
    for i in range(depth):
        j = i // 2
        if i % 2 == 0:
            h = _conformer_layer(h, mix_norm[i], cv_w_pw1[j], cv_b_pw1[j], cv_w_dw[j], cv_b_dw[j],
                                 cv_ln_g[j], cv_ln_b[j], cv_w_pw2[j], cv_b_pw2[j])
        else:
            h = _deltanet_layer(h, mix_norm[i], dn_w_in[j], dn_w_conv[j], dn_a_log[j], dn_dt_bias[j],
                                dn_norm_w[j], dn_w_out[j])
        last = i == depth - 1
        h = _ffn(h.reshape(bsz * s, d), _row(ffn_norm[i]), w_gate_up, w_down, _row(final_norm), layer=i,
                 tm=512, th=512, final_norm=last).reshape(bsz, s, d)
    return h
```
